```python
import jax, jax.numpy as jnp
from jax import lax
import numpy as np

D_MODEL = 1024
BATCH = 8
SEQ = 4096
DEPTH = 2

N_MEM = 256
HEAD_DIM = 64
N_GMLP_GROUPS = 6
N_FOX_HEADS = 6
N_MEM_HEADS = 4
D_GMLP = N_GMLP_GROUPS * HEAD_DIM
D_FOX = N_FOX_HEADS * HEAD_DIM
D_MEMQ = N_MEM_HEADS * HEAD_DIM
D_MIX = D_GMLP + D_FOX + D_MEMQ
D_IN = 2 * D_GMLP + 3 * D_FOX + N_FOX_HEADS + D_MEMQ
CHUNK = 128
Q_BLOCK = 128
D_FF = 4 * D_MODEL
RMS_EPS = 1e-6
NEG_INF = -1e30

kernel_name = "hybrid_gmlp_fox_memxattn_block"


def rmsnorm(x, g):
    xf = x.astype(jnp.float32)
    y = xf * lax.rsqrt(jnp.mean(jnp.square(xf), axis=-1, keepdims=True) + RMS_EPS)
    return (y * g.astype(jnp.float32)).astype(x.dtype)


def gmlp_spatial_gating(u, v, w_s, b_s, v_norm_g):
    B, S, _ = u.shape
    n_chunks = S // CHUNK
    u = jax.nn.gelu(u)
    v = jax.nn.gelu(v).reshape(B, S, N_GMLP_GROUPS, HEAD_DIM)
    v = rmsnorm(v, v_norm_g.reshape(N_GMLP_GROUPS, HEAD_DIM))
    v = v.reshape(B, n_chunks, CHUNK, N_GMLP_GROUPS, HEAD_DIM)
    w_causal = jnp.tril(w_s)
    mixed = jnp.einsum('gts,bcsgd->bctgd', w_causal, v)
    mixed = mixed + b_s.T[None, None, :, :, None].astype(mixed.dtype)
    return u * mixed.reshape(B, S, D_GMLP)


def forgetting_attention(q, k, v, log_f):
    B, S, H, dh = q.shape
    scale = dh ** -0.5
    c = jnp.cumsum(log_f, axis=1).transpose(0, 2, 1)
    q = q.transpose(0, 2, 1, 3)
    k = k.transpose(0, 2, 1, 3)
    v = v.transpose(0, 2, 1, 3)
    outs = []
    for i in range(S // Q_BLOCK):
        start, end = i * Q_BLOCK, (i + 1) * Q_BLOCK
        qb = q[:, :, start:end]
        kb = k[:, :, :end]
        vb = v[:, :, :end]
        s = jnp.einsum('bhtd,bhsd->bhts', qb, kb).astype(jnp.float32) * scale
        s = s + c[:, :, start:end, None] - c[:, :, None, :end]
        mask = jnp.arange(end)[None, :] <= jnp.arange(start, end)[:, None]
        s = jnp.where(mask, s, NEG_INF)
        p = jax.nn.softmax(s, axis=-1).astype(vb.dtype)
        outs.append(jnp.einsum('bhts,bhsd->bthd', p, vb))
    o = jnp.concatenate(outs, axis=1)
    return o.reshape(B, S, H * dh)


def memory_cross_attention(qm, mem_n, w_kv):
    B, S, _ = qm.shape
    M = mem_n.shape[1]
    km, vm = jnp.split(mem_n @ w_kv, 2, axis=-1)
    q = qm.reshape(B, S, N_MEM_HEADS, HEAD_DIM)
    km = km.reshape(B, M, N_MEM_HEADS, HEAD_DIM)
    vm = vm.reshape(B, M, N_MEM_HEADS, HEAD_DIM)
    s = jnp.einsum('bthd,bmhd->bhtm', q, km).astype(jnp.float32) * (HEAD_DIM ** -0.5)
    p = jax.nn.softmax(s, axis=-1).astype(vm.dtype)
    o = jnp.einsum('bhtm,bmhd->bthd', p, vm)
    return o.reshape(B, S, D_MEMQ)


def setup_inputs(seed: int = 0) -> dict:
    key = jax.random.key(seed)
    ks = jax.random.split(key, 17)
    f32 = jnp.float32

    def nrm(k, shape, scale):
        return jax.random.normal(k, shape, f32) * scale

    def gain(k, shape):
        return 1.0 + 0.05 * jax.random.normal(k, shape, f32)

    return {
        "x": jax.random.normal(ks[0], (BATCH, SEQ, D_MODEL), f32),
        "mem": jax.random.normal(ks[1], (BATCH, N_MEM, D_MODEL), f32),
        "norm_pre_mix": gain(ks[2], (DEPTH, D_MODEL)),
        "norm_post_mix": gain(ks[3], (DEPTH, D_MODEL)),
        "norm_pre_ffn": gain(ks[4], (DEPTH, D_MODEL)),
        "norm_post_ffn": gain(ks[5], (DEPTH, D_MODEL)),
        "norm_mem": gain(ks[6], (DEPTH, D_MODEL)),
        "w_in": nrm(ks[7], (DEPTH, D_MODEL, D_IN), D_MODEL ** -0.5),
        "b_forget": 3.0 + 0.5 * jax.random.normal(ks[8], (DEPTH, N_FOX_HEADS), f32),
        "gmlp_v_norm": gain(ks[9], (DEPTH, D_GMLP)),
        "gmlp_w_s": nrm(ks[10], (DEPTH, N_GMLP_GROUPS, CHUNK, CHUNK), CHUNK ** -0.5),
        "gmlp_b_s": 1.0 + 0.1 * jax.random.normal(ks[11], (DEPTH, N_GMLP_GROUPS, CHUNK), f32),
        "w_mem_kv": nrm(ks[12], (DEPTH, D_MODEL, 2 * D_MEMQ), D_MODEL ** -0.5),
        "w_out": nrm(ks[13], (DEPTH, D_MIX, D_MODEL), D_MIX ** -0.5),
        "w_ff1": nrm(ks[14], (DEPTH, D_MODEL, D_FF), D_MODEL ** -0.5),
        "w_ff2": nrm(ks[15], (DEPTH, D_FF, D_MODEL), D_FF ** -0.5),
    }


def reference(x, mem, norm_pre_mix, norm_post_mix, norm_pre_ffn, norm_post_ffn, norm_mem,
              w_in, b_forget, gmlp_v_norm, gmlp_w_s, gmlp_b_s, w_mem_kv, w_out, w_ff1, w_ff2):
    splits = np.cumsum([D_GMLP, D_GMLP, D_FOX, D_FOX, D_FOX, N_FOX_HEADS]).tolist()
    B, S, _ = x.shape
    for l in range(DEPTH):
        h = rmsnorm(x, norm_pre_mix[l])
        z = h @ w_in[l]
        g_u, g_v, f_q, f_k, f_v, f_gate, m_q = jnp.split(z, splits, axis=-1)

        out_a = gmlp_spatial_gating(g_u, g_v, gmlp_w_s[l], gmlp_b_s[l], gmlp_v_norm[l])

        log_f = jax.nn.log_sigmoid((f_gate + b_forget[l]).astype(jnp.float32))
        out_b = forgetting_attention(
            f_q.reshape(B, S, N_FOX_HEADS, HEAD_DIM),
            f_k.reshape(B, S, N_FOX_HEADS, HEAD_DIM),
            f_v.reshape(B, S, N_FOX_HEADS, HEAD_DIM),
            log_f)

        mem_n = rmsnorm(mem, norm_mem[l])
        out_c = memory_cross_attention(m_q, mem_n, w_mem_kv[l])

        y = jnp.concatenate([out_a, out_b, out_c], axis=-1) @ w_out[l]
        x = x + rmsnorm(y, norm_post_mix[l])

        h = rmsnorm(x, norm_pre_ffn[l])
        f = jnp.square(jax.nn.relu(h @ w_ff1[l])) @ w_ff2[l]
        x = x + rmsnorm(f, norm_post_ffn[l])
    return x
```

```python
import functools
import math

import numpy as np
import jax
import jax.numpy as jnp
from jax import lax
from jax.experimental import pallas as pl
from jax.experimental.pallas import tpu as pltpu

F32 = jnp.float32
BF16 = jnp.bfloat16

HEAD_DIM = 64
LANES = 128
N_GMLP_GROUPS = 6
N_FOX_HEADS = 6
N_MEM_HEADS = 4
D_GMLP = N_GMLP_GROUPS * HEAD_DIM
D_FOX = N_FOX_HEADS * HEAD_DIM
D_MEMQ = N_MEM_HEADS * HEAD_DIM
CHUNK = 128
RMS_EPS = 1e-6
NEG_INF = -1e30
QK_SCALE = HEAD_DIM ** -0.5
N_SPLIT = 3

V7X_VMEM_BYTES = 64 * 1024 * 1024
VMEM_LIMIT = V7X_VMEM_BYTES - 8 * 1024 * 1024

TM_PROJ = 512
TQ = 512
TK = 512
TM_FFN = 512
FF_CHUNK = 1024


def _dot(a, b):
    return jnp.dot(a, b, preferred_element_type=F32)


def _dot_nt(a, b):
    return lax.dot_general(a, b, (((1,), (1,)), ((), ())), preferred_element_type=F32)


def _rmsnorm(x, g):
    return x * lax.rsqrt(jnp.mean(x * x, axis=-1, keepdims=True) + RMS_EPS) * g


def _gelu_tanh(x):
    c = math.sqrt(2.0 / math.pi)
    return 0.5 * x * (1.0 + jnp.tanh(c * (x + 0.044715 * (x * x * x))))


def _log_sigmoid(x):
    return -(jnp.maximum(-x, 0.0) + jnp.log1p(jnp.exp(-jnp.abs(x))))


def _split_bf16(x):
    pieces = []
    r = x
    for _ in range(N_SPLIT):
        p = r.astype(BF16)
        pieces.append(p)
        r = r - p.astype(F32)
    return pieces


def _memkv_kernel(mem_ref, g_ref, w_ref, o_ref):
    mn = _rmsnorm(mem_ref[0], g_ref[0]).astype(BF16)
    o_ref[0, 0] = _dot(mn, w_ref[0]).astype(BF16)


def _memkv(mem, norm_mem, w_kv_bf16):
    depth = w_kv_bf16.shape[0]
    b, m, d = mem.shape
    n = w_kv_bf16.shape[-1]
    return pl.pallas_call(
        _memkv_kernel,
        grid=(depth, b),
        in_specs=[
            pl.BlockSpec((1, m, d), lambda l, i: (i, 0, 0)),
            pl.BlockSpec((1, 1, d), lambda l, i: (l, 0, 0)),
            pl.BlockSpec((1, d, n), lambda l, i: (l, 0, 0)),
        ],
        out_specs=pl.BlockSpec((1, 1, m, n), lambda l, i: (l, i, 0, 0)),
        out_shape=jax.ShapeDtypeStruct((depth, b, m, n), BF16),
        compiler_params=pltpu.CompilerParams(dimension_semantics=("arbitrary", "arbitrary")),
        name="mem_kv",
    )(mem, norm_mem.reshape(depth, 1, d), w_kv_bf16)


def _mix_proj_kernel(x_ref, gpre_ref, wgm_ref, wfx_ref, wgate_ref, wmq_ref, bf_ref, gv_ref,
                     ws_ref, bs_ref, kv_ref, gmean_ref, place_ref,
                     oa_ref, oc_ref, q_ref, k_ref, v_ref, carry_ref, *, tm):
    n_chunks = tm // CHUNK

    @pl.when(pl.program_id(1) == 0)
    def _():
        carry_ref[...] = jnp.zeros_like(carry_ref)

    lane = lax.broadcasted_iota(jnp.int32, (CHUNK, LANES), 1)
    lane_t = lax.broadcasted_iota(jnp.int32, (tm, LANES), 1)
    lo_half_t = lane_t < HEAD_DIM
    row = lax.broadcasted_iota(jnp.int32, (CHUNK, CHUNK), 0)
    col = lax.broadcasted_iota(jnp.int32, (CHUNK, CHUNK), 1)
    causal = row >= col

    h = _rmsnorm(x_ref[0], gpre_ref[...]).astype(BF16)

    zg = _dot(h, wgm_ref[...])
    u = _gelu_tanh(zg[:, :D_GMLP])
    v = _gelu_tanh(zg[:, D_GMLP:])
    sq_hi, sq_lo = _split_bf16(v * v)[:2]
    ms = _dot(sq_hi, gmean_ref[...]) + _dot(sq_lo, gmean_ref[...])
    vn = (v * lax.rsqrt(ms + RMS_EPS) * gv_ref[...]).astype(BF16)
    w_pairs = []
    for p in range(N_GMLP_GROUPS // 2):
        w0 = jnp.where(causal, ws_ref[2 * p], 0.0).astype(BF16)
        w1 = jnp.where(causal, ws_ref[2 * p + 1], 0.0).astype(BF16)
        w_pairs.append(jnp.concatenate([w0, w1], axis=0))
    for c in range(n_chunks):
        rows = slice(c * CHUNK, (c + 1) * CHUNK)
        parts = []
        for p in range(N_GMLP_GROUPS // 2):
            r = _dot(w_pairs[p], vn[rows, p * LANES:(p + 1) * LANES])
            parts.append(jnp.where(lane < HEAD_DIM, r[:CHUNK], r[CHUNK:]))
        mixed = jnp.concatenate(parts, axis=1) + bs_ref[...]
        oa_ref[0, rows, :] = (u[rows] * mixed).astype(BF16)

    zm = _dot(h, wmq_ref[...]) * QK_SCALE
    kv = kv_ref[0, 0]
    for p in range(N_MEM_HEADS // 2):
        qp = zm[:, p * LANES:(p + 1) * LANES]
        kmp = kv[:, p * LANES:(p + 1) * LANES]
        vmp = kv[:, D_MEMQ + p * LANES:D_MEMQ + (p + 1) * LANES]
        outs = []
        for e in range(2):
            keep = lo_half_t if e == 0 else jnp.logical_not(lo_half_t)
            qh = jnp.where(keep, qp, 0.0).astype(BF16)
            s = _dot_nt(qh, kmp)
            pe = jnp.exp(s - jnp.max(s, axis=-1, keepdims=True))
            den = jnp.sum(pe, axis=-1, keepdims=True)
            outs.append(_dot(pe.astype(BF16), vmp) / den)
        oc_ref[0, :, p * LANES:(p + 1) * LANES] = jnp.where(lo_half_t, outs[0], outs[1]).astype(BF16)

    zf = _dot(h, wfx_ref[...])
    log_f = _log_sigmoid(_dot(h, wgate_ref[...]) + bf_ref[...])
    tril = jnp.where(causal, 1.0, 0.0).astype(BF16)
    carry = carry_ref[...]
    cums = []
    for c in range(n_chunks):
        pieces = _split_bf16(log_f[c * CHUNK:(c + 1) * CHUNK])
        cum3 = _dot(tril, jnp.concatenate(pieces, axis=1))
        cum = carry
        for i in range(N_SPLIT):
            cum = cum + cum3[:, i * LANES:(i + 1) * LANES]
        carry = cum[CHUNK - 1:CHUNK, :]
        cums.append(cum)
    carry_ref[...] = carry
    c_all = jnp.concatenate(cums, axis=0)
    extras = _dot(jnp.concatenate(_split_bf16(c_all), axis=1), place_ref[...])

    neg_one = jnp.full((tm, LANES), -1.0, F32)
    one = jnp.ones((tm, LANES), F32)
    zero = jnp.zeros((tm, LANES), F32)
    q_fill_even = jnp.where((lane_t >= HEAD_DIM) & (lane_t < HEAD_DIM + N_SPLIT), neg_one, zero)
    q_fill_odd = jnp.where(lane_t < N_SPLIT, neg_one, zero)
    v_fill_even = jnp.where(lane_t == HEAD_DIM, one, zero)
    v_fill_odd = jnp.where(lane_t == 0, one, zero)
    for p in range(N_FOX_HEADS // 2):
        cols = slice(p * LANES, (p + 1) * LANES)
        zq = zf[:, cols] * QK_SCALE
        zk = zf[:, D_FOX + p * LANES:D_FOX + (p + 1) * LANES]
        zv = zf[:, 2 * D_FOX + p * LANES:2 * D_FOX + (p + 1) * LANES]
        ex = extras[:, cols]
        q_ref[0, 2 * p] = jnp.where(lo_half_t, zq, q_fill_even).astype(BF16)
        q_ref[0, 2 * p + 1] = jnp.where(lo_half_t, q_fill_odd, zq).astype(BF16)
        k_ref[0, 2 * p] = jnp.where(lo_half_t, zk, ex).astype(BF16)
        k_ref[0, 2 * p + 1] = jnp.where(lo_half_t, ex, zk).astype(BF16)
        v_ref[0, 2 * p] = jnp.where(lo_half_t, zv, v_fill_even).astype(BF16)
        v_ref[0, 2 * p + 1] = jnp.where(lo_half_t, v_fill_odd, zv).astype(BF16)


def _placement_matrix():
    m = np.zeros((N_SPLIT * LANES, D_FOX), np.float32)
    for h in range(N_FOX_HEADS):
        base = (h // 2) * LANES + (HEAD_DIM if h % 2 == 0 else 0)
        for i in range(N_SPLIT):
            m[i * LANES + h, base + i] = 1.0
    return m


def _group_mean_matrix():
    g = np.arange(D_GMLP) // HEAD_DIM
    return (g[:, None] == g[None, :]).astype(np.float32) / HEAD_DIM


def _mix_proj(x, layer, kv_all, p):
    b, s, d = x.shape
    tm = min(TM_PROJ, s)
    assert s % tm == 0 and tm % CHUNK == 0
    m = kv_all.shape[2]
    const2 = lambda shape: pl.BlockSpec(shape, lambda i, j: (0, 0))
    head_spec = pl.BlockSpec((1, N_FOX_HEADS, tm, LANES), lambda i, j: (i, 0, j, 0))
    head_shape = jax.ShapeDtypeStruct((b, N_FOX_HEADS, s, LANES), BF16)
    return pl.pallas_call(
        functools.partial(_mix_proj_kernel, tm=tm),
        grid=(b, s // tm),
        in_specs=[
            pl.BlockSpec((1, tm, d), lambda i, j: (i, j, 0)),
            const2((1, d)),
            const2((d, 2 * D_GMLP)),
            const2((d, 3 * D_FOX)),
            const2((d, LANES)),
            const2((d, D_MEMQ)),
            const2((1, LANES)),
            const2((1, D_GMLP)),
            pl.BlockSpec((N_GMLP_GROUPS, CHUNK, CHUNK), lambda i, j: (0, 0, 0)),
            const2((CHUNK, D_GMLP)),
            pl.BlockSpec((1, 1, m, 2 * D_MEMQ), lambda i, j: (layer, i, 0, 0)),
            const2((D_GMLP, D_GMLP)),
            const2((N_SPLIT * LANES, D_FOX)),
        ],
        out_specs=[
            pl.BlockSpec((1, tm, D_GMLP), lambda i, j: (i, j, 0)),
            pl.BlockSpec((1, tm, D_MEMQ), lambda i, j: (i, j, 0)),
            head_spec, head_spec, head_spec,
        ],
        out_shape=[
            jax.ShapeDtypeStruct((b, s, D_GMLP), BF16),
            jax.ShapeDtypeStruct((b, s, D_MEMQ), BF16),
            head_shape, head_shape, head_shape,
        ],
        scratch_shapes=[pltpu.VMEM((1, LANES), F32)],
        compiler_params=pltpu.CompilerParams(
            dimension_semantics=("arbitrary", "arbitrary"), vmem_limit_bytes=VMEM_LIMIT),
        name="mix_proj",
    )(x, p["g_pre"], p["w_gm"], p["w_fx"], p["w_gate"], p["w_mq"], p["b_forget"], p["g_v"],
      p["w_s"], p["b_s"], kv_all, p["gmean"], p["place"])


def _fox_attn_kernel(q_ref, k_ref, v_ref, o_ref, *, tq, tk):
    qi = pl.program_id(2)
    n_full = qi * (tq // tk)
    rows = lax.broadcasted_iota(jnp.int32, (tq, tk), 0)
    cols = lax.broadcasted_iota(jnp.int32, (tq, tk), 1)
    lane = lax.broadcasted_iota(jnp.int32, (tq, LANES), 1)

    def step(j, q, hh, m, acc, masked_offset):
        kb = k_ref[0, hh, pl.ds(pl.multiple_of(j * tk, tk), tk), :]
        vb = v_ref[0, hh, pl.ds(pl.multiple_of(j * tk, tk), tk), :]
        s = _dot_nt(q, kb)
        if masked_offset is not None:
            s = jnp.where(cols + masked_offset <= rows, s, NEG_INF)
        m_new = jnp.maximum(m, jnp.max(s, axis=-1, keepdims=True))
        alpha = jnp.exp(m - m_new)
        pe = jnp.exp(s - m_new).astype(BF16)
        return m_new, alpha * acc + _dot(pe, vb)

    outs = []
    for hh in range(2):
        q = q_ref[0, hh]
        m0 = jnp.full((tq, 1), NEG_INF, F32)
        acc0 = jnp.zeros((tq, LANES), F32)
        m, acc = lax.fori_loop(
            0, n_full, lambda j, c: step(j, q, hh, c[0], c[1], None), (m0, acc0))
        for d in range(tq // tk):
            m, acc = step(n_full + d, q, hh, m, acc, d * tk)
        den_lane = HEAD_DIM if hh == 0 else 0
        den = jnp.sum(jnp.where(lane == den_lane, acc, 0.0), axis=-1, keepdims=True)
        outs.append(acc / den)
    o_ref[0] = jnp.where(lane < HEAD_DIM, outs[0], outs[1]).astype(BF16)


def _fox_attn(q, k, v):
    b, nh, s, _ = q.shape
    tq = min(TQ, s)
    tk = min(TK, tq)
    assert s % tq == 0 and tq % tk == 0
    q_spec = pl.BlockSpec((1, 2, tq, LANES), lambda i, p, j: (i, p, j, 0))
    kv_spec = pl.BlockSpec((1, 2, s, LANES), lambda i, p, j: (i, p, 0, 0))
    return pl.pallas_call(
        functools.partial(_fox_attn_kernel, tq=tq, tk=tk),
        grid=(b, nh // 2, s // tq),
        in_specs=[q_spec, kv_spec, kv_spec],
        out_specs=pl.BlockSpec((1, tq, LANES), lambda i, p, j: (i, j, p)),
        out_shape=jax.ShapeDtypeStruct((b, s, nh * HEAD_DIM), BF16),
        compiler_params=pltpu.CompilerParams(
            dimension_semantics=("arbitrary", "arbitrary", "arbitrary"), vmem_limit_bytes=VMEM_LIMIT),
        name="fox_attn",
    )(q, k, v)


def _out_ffn_kernel(x_ref, a_ref, b_ref, c_ref, woa_ref, wob_ref, woc_ref, g1_ref, g2_ref, g3_ref,
                    w1_ref, w2_ref, o_ref, *, ff_chunk):
    y = _dot(a_ref[...], woa_ref[...]) + _dot(b_ref[...], wob_ref[...]) + _dot(c_ref[...], woc_ref[...])
    x1 = x_ref[...] + _rmsnorm(y, g1_ref[...])
    h = _rmsnorm(x1, g2_ref[...]).astype(BF16)
    d_ff = w1_ref.shape[1]
    f = None
    for c in range(d_ff // ff_chunk):
        cols = slice(c * ff_chunk, (c + 1) * ff_chunk)
        a1 = jnp.maximum(_dot(h, w1_ref[:, cols]), 0.0)
        part = _dot((a1 * a1).astype(BF16), w2_ref[cols, :])
        f = part if f is None else f + part
    o_ref[...] = x1 + _rmsnorm(f, g3_ref[...])


def _out_ffn(x, a, b_, c, p):
    t, d = x.shape
    tm = min(TM_FFN, t)
    assert t % tm == 0
    d_ff = p["w_ff1"].shape[1]
    ff_chunk = min(FF_CHUNK, d_ff)
    row = lambda w: pl.BlockSpec((tm, w), lambda i: (i, 0))
    const = lambda shape: pl.BlockSpec(shape, lambda i: (0, 0), pipeline_mode=pl.Buffered(1))
    return pl.pallas_call(
        functools.partial(_out_ffn_kernel, ff_chunk=ff_chunk),
        grid=(t // tm,),
        in_specs=[
            row(d), row(D_GMLP), row(D_FOX), row(D_MEMQ),
            const((D_GMLP, d)), const((D_FOX, d)), const((D_MEMQ, d)),
            const((1, d)), const((1, d)), const((1, d)),
            const((d, d_ff)), const((d_ff, d)),
        ],
        out_specs=row(d),
        out_shape=jax.ShapeDtypeStruct((t, d), F32),
        compiler_params=pltpu.CompilerParams(
            dimension_semantics=("arbitrary",), vmem_limit_bytes=VMEM_LIMIT),
        name="out_ffn",
    )(x, a, b_, c, p["wo_a"], p["wo_b"], p["wo_c"], p["g_post_mix"], p["g_pre_ffn"], p["g_post_ffn"],
      p["w_ff1"], p["w_ff2"])


def _layer_params(l, norm_pre_mix, norm_post_mix, norm_pre_ffn, norm_post_ffn, w_in, b_forget,
                  gmlp_v_norm, gmlp_w_s, gmlp_b_s, w_out, w_ff1, w_ff2):
    d = w_in.shape[1]
    w = w_in[l]
    o_fx = 2 * D_GMLP
    o_gate = o_fx + 3 * D_FOX
    o_mq = o_gate + N_FOX_HEADS
    pad = LANES - N_FOX_HEADS
    wo = w_out[l].astype(BF16)
    return {
        "g_pre": norm_pre_mix[l].reshape(1, d),
        "w_gm": w[:, :o_fx].astype(BF16),
        "w_fx": w[:, o_fx:o_gate].astype(BF16),
        "w_gate": jnp.pad(w[:, o_gate:o_mq], ((0, 0), (0, pad))).astype(BF16),
        "w_mq": w[:, o_mq:].astype(BF16),
        "b_forget": jnp.pad(b_forget[l], (0, pad)).reshape(1, LANES),
        "g_v": gmlp_v_norm[l].reshape(1, D_GMLP),
        "w_s": gmlp_w_s[l],
        "b_s": jnp.repeat(gmlp_b_s[l].T, HEAD_DIM, axis=1),
        "gmean": jnp.asarray(_group_mean_matrix(), BF16),
        "place": jnp.asarray(_placement_matrix(), BF16),
        "wo_a": wo[:D_GMLP],
        "wo_b": wo[D_GMLP:D_GMLP + D_FOX],
        "wo_c": wo[D_GMLP + D_FOX:],
        "g_post_mix": norm_post_mix[l].reshape(1, d),
        "g_pre_ffn": norm_pre_ffn[l].reshape(1, d),
        "g_post_ffn": norm_post_ffn[l].reshape(1, d),
        "w_ff1": w_ff1[l].astype(BF16),
        "w_ff2": w_ff2[l].astype(BF16),
    }


def kernel(x, mem, norm_pre_mix, norm_post_mix, norm_pre_ffn, norm_post_ffn, norm_mem, w_in, b_forget,
           gmlp_v_norm, gmlp_w_s, gmlp_b_s, w_mem_kv, w_out, w_ff1, w_ff2):
    b, s, d = x.shape
    depth = w_in.shape[0]
    kv_all = _memkv(mem, norm_mem, w_mem_kv.astype(BF16))
    for l in range(depth):
        p = _layer_params(l, norm_pre_mix, norm_post_mix, norm_pre_ffn, norm_post_ffn, w_in, b_forget,
                          gmlp_v_norm, gmlp_w_s, gmlp_b_s, w_out, w_ff1, w_ff2)
        out_a, out_c, q, k, v = _mix_proj(x, l, kv_all, p)
        out_b = _fox_attn(q, k, v)
        x = _out_ffn(x.reshape(b * s, d), out_a.reshape(b * s, D_GMLP), out_b.reshape(b * s, D_FOX),
                     out_c.reshape(b * s, D_MEMQ), p).reshape(b, s, d)
    return x
```

```python
import functools
import math

import numpy as np
import jax
import jax.numpy as jnp
from jax import lax
from jax.experimental import pallas as pl
from jax.experimental.pallas import tpu as pltpu

F32 = jnp.float32
BF16 = jnp.bfloat16

HEAD_DIM = 64
LANES = 128
N_GMLP_GROUPS = 6
N_FOX_HEADS = 6
N_MEM_HEADS = 4
D_GMLP = N_GMLP_GROUPS * HEAD_DIM
D_FOX = N_FOX_HEADS * HEAD_DIM
D_MEMQ = N_MEM_HEADS * HEAD_DIM
CHUNK = 128
RMS_EPS = 1e-6
NEG_INF = -1e30
QK_SCALE = HEAD_DIM ** -0.5
N_SPLIT = 3

V7X_VMEM_BYTES = 64 * 1024 * 1024
VMEM_LIMIT = V7X_VMEM_BYTES - 8 * 1024 * 1024

TM_PROJ = 512
TQ = 512
TK = 512
HEADS_PER_STEP = 6
TM_FFN = 512
FF_CHUNK = 1024


def _dot(a, b):
    return jnp.dot(a, b, preferred_element_type=F32)


def _dot_nt(a, b):
    return lax.dot_general(a, b, (((1,), (1,)), ((), ())), preferred_element_type=F32)


def _rmsnorm(x, g):
    return x * lax.rsqrt(jnp.mean(x * x, axis=-1, keepdims=True) + RMS_EPS) * g


def _gelu_tanh(x):
    c = math.sqrt(2.0 / math.pi)
    return 0.5 * x * (1.0 + jnp.tanh(c * (x + 0.044715 * (x * x * x))))


def _log_sigmoid(x):
    return -(jnp.maximum(-x, 0.0) + jnp.log1p(jnp.exp(-jnp.abs(x))))


def _split_bf16(x):
    pieces = []
    r = x
    for _ in range(N_SPLIT):
        p = r.astype(BF16)
        pieces.append(p)
        r = r - p.astype(F32)
    return pieces


def _memkv_kernel(mem_ref, g_ref, w_ref, o_ref):
    mn = _rmsnorm(mem_ref[0], g_ref[0]).astype(BF16)
    o_ref[0, 0] = _dot(mn, w_ref[0]).astype(BF16)


def _memkv(mem, norm_mem, w_kv_bf16):
    depth = w_kv_bf16.shape[0]
    b, m, d = mem.shape
    n = w_kv_bf16.shape[-1]
    return pl.pallas_call(
        _memkv_kernel,
        grid=(depth, b),
        in_specs=[
            pl.BlockSpec((1, m, d), lambda l, i: (i, 0, 0)),
            pl.BlockSpec((1, 1, d), lambda l, i: (l, 0, 0)),
            pl.BlockSpec((1, d, n), lambda l, i: (l, 0, 0)),
        ],
        out_specs=pl.BlockSpec((1, 1, m, n), lambda l, i: (l, i, 0, 0)),
        out_shape=jax.ShapeDtypeStruct((depth, b, m, n), BF16),
        compiler_params=pltpu.CompilerParams(dimension_semantics=("arbitrary", "arbitrary")),
        name="mem_kv",
    )(mem, norm_mem.reshape(depth, 1, d), w_kv_bf16)


def _mix_proj_kernel(x_ref, gpre_ref, wgm_ref, wfx_ref, wgate_ref, wmq_ref, bf_ref, gv_ref,
                     ws_ref, bs_ref, kv_ref, gmean_ref, place_ref,
                     oa_ref, oc_ref, q_ref, k_ref, v_ref, carry_ref, *, tm):
    n_chunks = tm // CHUNK

    @pl.when(pl.program_id(1) == 0)
    def _():
        carry_ref[...] = jnp.zeros_like(carry_ref)

    lane = lax.broadcasted_iota(jnp.int32, (CHUNK, LANES), 1)
    lane_t = lax.broadcasted_iota(jnp.int32, (tm, LANES), 1)
    lo_half_t = lane_t < HEAD_DIM
    row = lax.broadcasted_iota(jnp.int32, (CHUNK, CHUNK), 0)
    col = lax.broadcasted_iota(jnp.int32, (CHUNK, CHUNK), 1)
    causal = row >= col

    h = _rmsnorm(x_ref[0], gpre_ref[...]).astype(BF16)

    zg = _dot(h, wgm_ref[...])
    u = _gelu_tanh(zg[:, :D_GMLP])
    v = _gelu_tanh(zg[:, D_GMLP:])
    sq_hi, sq_lo = _split_bf16(v * v)[:2]
    ms = _dot(sq_hi, gmean_ref[...]) + _dot(sq_lo, gmean_ref[...])
    vn = (v * lax.rsqrt(ms + RMS_EPS) * gv_ref[...]).astype(BF16)
    w_pairs = []
    for p in range(N_GMLP_GROUPS // 2):
        w0 = jnp.where(causal, ws_ref[2 * p], 0.0).astype(BF16)
        w1 = jnp.where(causal, ws_ref[2 * p + 1], 0.0).astype(BF16)
        w_pairs.append(jnp.concatenate([w0, w1], axis=0))
    for c in range(n_chunks):
        rows = slice(c * CHUNK, (c + 1) * CHUNK)
        parts = []
        for p in range(N_GMLP_GROUPS // 2):
            r = _dot(w_pairs[p], vn[rows, p * LANES:(p + 1) * LANES])
            parts.append(jnp.where(lane < HEAD_DIM, r[:CHUNK], r[CHUNK:]))
        mixed = jnp.concatenate(parts, axis=1) + bs_ref[...]
        oa_ref[0, rows, :] = (u[rows] * mixed).astype(BF16)

    zm = _dot(h, wmq_ref[...]) * QK_SCALE
    kv = kv_ref[0, 0]
    for p in range(N_MEM_HEADS // 2):
        qp = zm[:, p * LANES:(p + 1) * LANES]
        kmp = kv[:, p * LANES:(p + 1) * LANES]
        vmp = kv[:, D_MEMQ + p * LANES:D_MEMQ + (p + 1) * LANES]
        outs = []
        for e in range(2):
            keep = lo_half_t if e == 0 else jnp.logical_not(lo_half_t)
            qh = jnp.where(keep, qp, 0.0).astype(BF16)
            s = _dot_nt(qh, kmp)
            pe = jnp.exp(s - jnp.max(s, axis=-1, keepdims=True))
            den = jnp.sum(pe, axis=-1, keepdims=True)
            outs.append(_dot(pe.astype(BF16), vmp) / den)
        oc_ref[0, :, p * LANES:(p + 1) * LANES] = jnp.where(lo_half_t, outs[0], outs[1]).astype(BF16)

    zf = _dot(h, wfx_ref[...])
    log_f = _log_sigmoid(_dot(h, wgate_ref[...]) + bf_ref[...])
    tril = jnp.where(causal, 1.0, 0.0).astype(BF16)
    carry = carry_ref[...]
    cums = []
    for c in range(n_chunks):
        pieces = _split_bf16(log_f[c * CHUNK:(c + 1) * CHUNK])
        cum3 = _dot(tril, jnp.concatenate(pieces, axis=1))
        cum = carry
        for i in range(N_SPLIT):
            cum = cum + cum3[:, i * LANES:(i + 1) * LANES]
        carry = cum[CHUNK - 1:CHUNK, :]
        cums.append(cum)
    carry_ref[...] = carry
    c_all = jnp.concatenate(cums, axis=0)
    extras = _dot(jnp.concatenate(_split_bf16(c_all), axis=1), place_ref[...])

    neg_one = jnp.full((tm, LANES), -1.0, F32)
    one = jnp.ones((tm, LANES), F32)
    zero = jnp.zeros((tm, LANES), F32)
    q_fill_even = jnp.where((lane_t >= HEAD_DIM) & (lane_t < HEAD_DIM + N_SPLIT), neg_one, zero)
    q_fill_odd = jnp.where(lane_t < N_SPLIT, neg_one, zero)
    v_fill_even = jnp.where(lane_t == HEAD_DIM, one, zero)
    v_fill_odd = jnp.where(lane_t == 0, one, zero)
    for p in range(N_FOX_HEADS // 2):
        cols = slice(p * LANES, (p + 1) * LANES)
        zq = zf[:, cols] * QK_SCALE
        zk = zf[:, D_FOX + p * LANES:D_FOX + (p + 1) * LANES]
        zv = zf[:, 2 * D_FOX + p * LANES:2 * D_FOX + (p + 1) * LANES]
        ex = extras[:, cols]
        q_ref[0, 2 * p] = jnp.where(lo_half_t, zq, q_fill_even).astype(BF16)
        q_ref[0, 2 * p + 1] = jnp.where(lo_half_t, q_fill_odd, zq).astype(BF16)
        k_ref[0, 2 * p] = jnp.where(lo_half_t, zk, ex).astype(BF16)
        k_ref[0, 2 * p + 1] = jnp.where(lo_half_t, ex, zk).astype(BF16)
        v_ref[0, 2 * p] = jnp.where(lo_half_t, zv, v_fill_even).T.astype(BF16)
        v_ref[0, 2 * p + 1] = jnp.where(lo_half_t, v_fill_odd, zv).T.astype(BF16)


def _placement_matrix():
    m = np.zeros((N_SPLIT * LANES, D_FOX), np.float32)
    for h in range(N_FOX_HEADS):
        base = (h // 2) * LANES + (HEAD_DIM if h % 2 == 0 else 0)
        for i in range(N_SPLIT):
            m[i * LANES + h, base + i] = 1.0
    return m


def _group_mean_matrix():
    g = np.arange(D_GMLP) // HEAD_DIM
    return (g[:, None] == g[None, :]).astype(np.float32) / HEAD_DIM


def _mix_proj(x, layer, kv_all, p):
    b, s, d = x.shape
    tm = min(TM_PROJ, s)
    assert s % tm == 0 and tm % CHUNK == 0
    m = kv_all.shape[2]
    const2 = lambda shape: pl.BlockSpec(shape, lambda i, j: (0, 0))
    head_spec = pl.BlockSpec((1, N_FOX_HEADS, tm, LANES), lambda i, j: (i, 0, j, 0))
    head_shape = jax.ShapeDtypeStruct((b, N_FOX_HEADS, s, LANES), BF16)
    return pl.pallas_call(
        functools.partial(_mix_proj_kernel, tm=tm),
        grid=(b, s // tm),
        in_specs=[
            pl.BlockSpec((1, tm, d), lambda i, j: (i, j, 0)),
            const2((1, d)),
            const2((d, 2 * D_GMLP)),
            const2((d, 3 * D_FOX)),
            const2((d, LANES)),
            const2((d, D_MEMQ)),
            const2((1, LANES)),
            const2((1, D_GMLP)),
            pl.BlockSpec((N_GMLP_GROUPS, CHUNK, CHUNK), lambda i, j: (0, 0, 0)),
            const2((CHUNK, D_GMLP)),
            pl.BlockSpec((1, 1, m, 2 * D_MEMQ), lambda i, j: (layer, i, 0, 0)),
            const2((D_GMLP, D_GMLP)),
            const2((N_SPLIT * LANES, D_FOX)),
        ],
        out_specs=[
            pl.BlockSpec((1, tm, D_GMLP), lambda i, j: (i, j, 0)),
            pl.BlockSpec((1, tm, D_MEMQ), lambda i, j: (i, j, 0)),
            head_spec, head_spec,
            pl.BlockSpec((1, N_FOX_HEADS, LANES, tm), lambda i, j: (i, 0, 0, j)),
        ],
        out_shape=[
            jax.ShapeDtypeStruct((b, s, D_GMLP), BF16),
            jax.ShapeDtypeStruct((b, s, D_MEMQ), BF16),
            head_shape, head_shape,
            jax.ShapeDtypeStruct((b, N_FOX_HEADS, LANES, s), BF16),
        ],
        scratch_shapes=[pltpu.VMEM((1, LANES), F32)],
        compiler_params=pltpu.CompilerParams(
            dimension_semantics=("arbitrary", "arbitrary"), vmem_limit_bytes=VMEM_LIMIT),
        name="mix_proj",
    )(x, p["g_pre"], p["w_gm"], p["w_fx"], p["w_gate"], p["w_mq"], p["b_forget"], p["g_v"],
      p["w_s"], p["b_s"], kv_all, p["gmean"], p["place"])


def _fox_attn_kernel(q_ref, k_ref, vt_ref, o_ref, *, tq, tk):
    assert tq == tk
    hps = q_ref.shape[1]
    n_full = pl.program_id(2)
    key_idx = lax.broadcasted_iota(jnp.int32, (tk, tq), 0)
    qry_idx = lax.broadcasted_iota(jnp.int32, (tk, tq), 1)
    sub = lax.broadcasted_iota(jnp.int32, (LANES, tq), 0)

    qs = [q_ref[0, hh] for hh in range(hps)]

    def scores(j):
        start = pl.multiple_of(j * tk, tk)
        return tuple(_dot_nt(k_ref[0, hh, pl.ds(start, tk), :], qs[hh]) for hh in range(hps))

    def consume(j, sts, carry, masked):
        start = pl.multiple_of(j * tk, tk)
        new = []
        for hh in range(hps):
            m, acc = carry[hh]
            vtb = vt_ref[0, hh, :, pl.ds(start, tk)]
            st = sts[hh]
            if masked:
                st = jnp.where(key_idx <= qry_idx, st, NEG_INF)
            m_new = jnp.maximum(m, jnp.max(st, axis=0, keepdims=True))
            alpha = jnp.exp(m - m_new)
            pt = jnp.exp(st - m_new).astype(BF16)
            new.append((m_new, alpha * acc + _dot(vtb, pt)))
        return tuple(new)

    init = tuple((jnp.full((1, tq), NEG_INF, F32), jnp.zeros((LANES, tq), F32)) for _ in range(hps))
    carry = lax.fori_loop(0, n_full, lambda j, c: consume(j, scores(j), c, False), init)
    carry = consume(n_full, scores(n_full), carry, True)
    for p in range(hps // 2):
        outs = []
        for e in range(2):
            acc = carry[2 * p + e][1]
            den_row = HEAD_DIM if e == 0 else 0
            outs.append(acc / acc[den_row:den_row + 1, :])
        pair_t = jnp.where(sub < HEAD_DIM, outs[0], outs[1])
        o_ref[0, :, p * LANES:(p + 1) * LANES] = pair_t.T.astype(BF16)


def _fox_attn(q, k, v):
    b, nh, s, _ = q.shape
    tq = min(TQ, s)
    tk = min(TK, tq)
    hps = HEADS_PER_STEP
    assert s % tq == 0 and tq % tk == 0 and nh % hps == 0 and hps % 2 == 0
    q_spec = pl.BlockSpec((1, hps, tq, LANES), lambda i, p, j: (i, p, j, 0))
    k_spec = pl.BlockSpec((1, hps, s, LANES), lambda i, p, j: (i, p, 0, 0))
    vt_spec = pl.BlockSpec((1, hps, LANES, s), lambda i, p, j: (i, p, 0, 0))
    return pl.pallas_call(
        functools.partial(_fox_attn_kernel, tq=tq, tk=tk),
        grid=(b, nh // hps, s // tq),
        in_specs=[q_spec, k_spec, vt_spec],
        out_specs=pl.BlockSpec((1, tq, hps * HEAD_DIM), lambda i, p, j: (i, j, p)),
        out_shape=jax.ShapeDtypeStruct((b, s, nh * HEAD_DIM), BF16),
        compiler_params=pltpu.CompilerParams(
            dimension_semantics=("arbitrary", "arbitrary", "arbitrary"), vmem_limit_bytes=VMEM_LIMIT),
        name="fox_attn",
    )(q, k, v)


def _out_ffn_kernel(x_ref, a_ref, b_ref, c_ref, woa_ref, wob_ref, woc_ref, g1_ref, g2_ref, g3_ref,
                    w1_ref, w2_ref, o_ref, *, ff_chunk):
    y = _dot(a_ref[...], woa_ref[...]) + _dot(b_ref[...], wob_ref[...]) + _dot(c_ref[...], woc_ref[...])
    x1 = x_ref[...] + _rmsnorm(y, g1_ref[...])
    h = _rmsnorm(x1, g2_ref[...]).astype(BF16)
    d_ff = w1_ref.shape[1]
    f = None
    for c in range(d_ff // ff_chunk):
        cols = slice(c * ff_chunk, (c + 1) * ff_chunk)
        a1 = jnp.maximum(_dot(h, w1_ref[:, cols]), 0.0)
        part = _dot((a1 * a1).astype(BF16), w2_ref[cols, :])
        f = part if f is None else f + part
    o_ref[...] = x1 + _rmsnorm(f, g3_ref[...])


def _out_ffn(x, a, b_, c, p):
    t, d = x.shape
    tm = min(TM_FFN, t)
    assert t % tm == 0
    d_ff = p["w_ff1"].shape[1]
    ff_chunk = min(FF_CHUNK, d_ff)
    row = lambda w: pl.BlockSpec((tm, w), lambda i: (i, 0))
    const = lambda shape: pl.BlockSpec(shape, lambda i: (0, 0), pipeline_mode=pl.Buffered(1))
    return pl.pallas_call(
        functools.partial(_out_ffn_kernel, ff_chunk=ff_chunk),
        grid=(t // tm,),
        in_specs=[
            row(d), row(D_GMLP), row(D_FOX), row(D_MEMQ),
            const((D_GMLP, d)), const((D_FOX, d)), const((D_MEMQ, d)),
            const((1, d)), const((1, d)), const((1, d)),
            const((d, d_ff)), const((d_ff, d)),
        ],
        out_specs=row(d),
        out_shape=jax.ShapeDtypeStruct((t, d), F32),
        compiler_params=pltpu.CompilerParams(
            dimension_semantics=("arbitrary",), vmem_limit_bytes=VMEM_LIMIT),
        name="out_ffn",
    )(x, a, b_, c, p["wo_a"], p["wo_b"], p["wo_c"], p["g_post_mix"], p["g_pre_ffn"], p["g_post_ffn"],
      p["w_ff1"], p["w_ff2"])


def _layer_params(l, norm_pre_mix, norm_post_mix, norm_pre_ffn, norm_post_ffn, w_in, b_forget,
                  gmlp_v_norm, gmlp_w_s, gmlp_b_s, w_out, w_ff1, w_ff2):
    d = w_in.shape[1]
    w = w_in[l]
    o_fx = 2 * D_GMLP
    o_gate = o_fx + 3 * D_FOX
    o_mq = o_gate + N_FOX_HEADS
    pad = LANES - N_FOX_HEADS
    wo = w_out[l].astype(BF16)
    return {
        "g_pre": norm_pre_mix[l].reshape(1, d),
        "w_gm": w[:, :o_fx].astype(BF16),
        "w_fx": w[:, o_fx:o_gate].astype(BF16),
        "w_gate": jnp.pad(w[:, o_gate:o_mq], ((0, 0), (0, pad))).astype(BF16),
        "w_mq": w[:, o_mq:].astype(BF16),
        "b_forget": jnp.pad(b_forget[l], (0, pad)).reshape(1, LANES),
        "g_v": gmlp_v_norm[l].reshape(1, D_GMLP),
        "w_s": gmlp_w_s[l],
        "b_s": jnp.repeat(gmlp_b_s[l].T, HEAD_DIM, axis=1),
        "gmean": jnp.asarray(_group_mean_matrix(), BF16),
        "place": jnp.asarray(_placement_matrix(), BF16),
        "wo_a": wo[:D_GMLP],
        "wo_b": wo[D_GMLP:D_GMLP + D_FOX],
        "wo_c": wo[D_GMLP + D_FOX:],
        "g_post_mix": norm_post_mix[l].reshape(1, d),
        "g_pre_ffn": norm_pre_ffn[l].reshape(1, d),
        "g_post_ffn": norm_post_ffn[l].reshape(1, d),
        "w_ff1": w_ff1[l].astype(BF16),
        "w_ff2": w_ff2[l].astype(BF16),
    }


def kernel(x, mem, norm_pre_mix, norm_post_mix, norm_pre_ffn, norm_post_ffn, norm_mem, w_in, b_forget,
           gmlp_v_norm, gmlp_w_s, gmlp_b_s, w_mem_kv, w_out, w_ff1, w_ff2):
    b, s, d = x.shape
    depth = w_in.shape[0]
    kv_all = _memkv(mem, norm_mem, w_mem_kv.astype(BF16))
    for l in range(depth):
        p = _layer_params(l, norm_pre_mix, norm_post_mix, norm_pre_ffn, norm_post_ffn, w_in, b_forget,
                          gmlp_v_norm, gmlp_w_s, gmlp_b_s, w_out, w_ff1, w_ff2)
        out_a, out_c, q, k, v = _mix_proj(x, l, kv_all, p)
        out_b = _fox_attn(q, k, v)
        x = _out_ffn(x.reshape(b * s, d), out_a.reshape(b * s, D_GMLP), out_b.reshape(b * s, D_FOX),
                     out_c.reshape(b * s, D_MEMQ), p).reshape(b, s, d)
    return x
```

```python
import functools
import math

import numpy as np
import jax
import jax.numpy as jnp
from jax import lax
from jax.experimental import pallas as pl
from jax.experimental.pallas import tpu as pltpu

F32 = jnp.float32
BF16 = jnp.bfloat16

HEAD_DIM = 64
LANES = 128
N_GMLP_GROUPS = 6
N_FOX_HEADS = 6
N_MEM_HEADS = 4
D_GMLP = N_GMLP_GROUPS * HEAD_DIM
D_FOX = N_FOX_HEADS * HEAD_DIM
D_MEMQ = N_MEM_HEADS * HEAD_DIM
CHUNK = 128
RMS_EPS = 1e-6
NEG_INF = -1e30
QK_SCALE = HEAD_DIM ** -0.5
N_SPLIT = 3
O_MQ = 2 * D_GMLP
O_GATE = O_MQ + D_MEMQ
O_FX = O_GATE + LANES
W_ALL = O_FX + 3 * D_FOX

V7X_VMEM_BYTES = 64 * 1024 * 1024
VMEM_LIMIT = V7X_VMEM_BYTES - 8 * 1024 * 1024

TM_PROJ = 512
TQ = 512
TK = 512
HEADS_PER_STEP = 6
TM_FFN = 512
FF_CHUNK = 1024


def _dot(a, b):
    return jnp.dot(a, b, preferred_element_type=F32)


def _dot_nt(a, b):
    return lax.dot_general(a, b, (((1,), (1,)), ((), ())), preferred_element_type=F32)


def _rmsnorm(x, g):
    return x * lax.rsqrt(jnp.mean(x * x, axis=-1, keepdims=True) + RMS_EPS) * g


def _gelu_tanh(x):
    c = math.sqrt(2.0 / math.pi)
    return 0.5 * x * (1.0 + jnp.tanh(c * (x + 0.044715 * (x * x * x))))


def _log_sigmoid(x):
    return -(jnp.maximum(-x, 0.0) + jnp.log1p(jnp.exp(-jnp.abs(x))))


def _split_bf16(x):
    pieces = []
    r = x
    for _ in range(N_SPLIT):
        p = r.astype(BF16)
        pieces.append(p)
        r = r - p.astype(F32)
    return pieces


def _memkv_kernel(mem_ref, g_ref, w_ref, o_ref):
    mn = _rmsnorm(mem_ref[0], g_ref[0]).astype(BF16)
    o_ref[0, 0] = _dot(mn, w_ref[0]).astype(BF16)


def _memkv(mem, norm_mem, w_kv_bf16):
    depth = w_kv_bf16.shape[0]
    b, m, d = mem.shape
    n = w_kv_bf16.shape[-1]
    return pl.pallas_call(
        _memkv_kernel,
        grid=(depth, b),
        in_specs=[
            pl.BlockSpec((1, m, d), lambda l, i: (i, 0, 0)),
            pl.BlockSpec((1, 1, d), lambda l, i: (l, 0, 0)),
            pl.BlockSpec((1, d, n), lambda l, i: (l, 0, 0)),
        ],
        out_specs=pl.BlockSpec((1, 1, m, n), lambda l, i: (l, i, 0, 0)),
        out_shape=jax.ShapeDtypeStruct((depth, b, m, n), BF16),
        compiler_params=pltpu.CompilerParams(dimension_semantics=("arbitrary", "arbitrary")),
        name="mem_kv",
    )(mem, norm_mem.reshape(depth, 1, d), w_kv_bf16)


def _mix_proj_kernel(x_ref, gpre_ref, wall_ref, bf_ref, gv_ref, ws_ref, bs_ref, kv_ref, gmean_ref, place_ref,
                     oa_ref, oc_ref, q_ref, k_ref, v_ref, carry_ref, *, tm):
    n_chunks = tm // CHUNK
    assert n_chunks % 2 == 0

    @pl.when(pl.program_id(1) == 0)
    def _():
        carry_ref[...] = jnp.zeros_like(carry_ref)

    lane = lax.broadcasted_iota(jnp.int32, (CHUNK, LANES), 1)
    lane_t = lax.broadcasted_iota(jnp.int32, (tm, LANES), 1)
    lo_half_t = lane_t < HEAD_DIM
    row = lax.broadcasted_iota(jnp.int32, (CHUNK, CHUNK), 0)
    col = lax.broadcasted_iota(jnp.int32, (CHUNK, CHUNK), 1)
    causal = row >= col

    h = _rmsnorm(x_ref[0], gpre_ref[...]).astype(BF16)
    z = _dot(h, wall_ref[...])
    zg = z[:, :O_MQ]
    zm = z[:, O_MQ:O_GATE] * QK_SCALE
    zgate = z[:, O_GATE:O_FX]
    zf = z[:, O_FX:]

    u = _gelu_tanh(zg[:, :D_GMLP])
    v = _gelu_tanh(zg[:, D_GMLP:])
    ms = _dot((v * v).astype(BF16), gmean_ref[...])

    kv = kv_ref[0, 0]
    mem_scores = []
    for hd in range(N_MEM_HEADS):
        p, e = divmod(hd, 2)
        keep = lo_half_t if e == 0 else jnp.logical_not(lo_half_t)
        qh = jnp.where(keep, zm[:, p * LANES:(p + 1) * LANES], 0.0).astype(BF16)
        mem_scores.append(_dot_nt(qh, kv[:, p * LANES:(p + 1) * LANES]))

    log_f = _log_sigmoid(zgate + bf_ref[...])
    tril = jnp.where(causal, 1.0, 0.0).astype(BF16)
    carry = carry_ref[...]
    cums = []
    for c in range(n_chunks):
        pieces = _split_bf16(log_f[c * CHUNK:(c + 1) * CHUNK])
        cum3 = _dot(tril, jnp.concatenate(pieces, axis=1))
        cum = carry
        for i in range(N_SPLIT):
            cum = cum + cum3[:, i * LANES:(i + 1) * LANES]
        carry = cum[CHUNK - 1:CHUNK, :]
        cums.append(cum)
    carry_ref[...] = carry
    c_all = jnp.concatenate(cums, axis=0)

    vn = (v * lax.rsqrt(ms + RMS_EPS) * gv_ref[...]).astype(BF16)
    w_pairs = []
    for p in range(N_GMLP_GROUPS // 2):
        w0 = jnp.where(causal, ws_ref[2 * p], 0.0).astype(BF16)
        w1 = jnp.where(causal, ws_ref[2 * p + 1], 0.0).astype(BF16)
        w_pairs.append(jnp.concatenate([w0, w1], axis=0))
    for cp in range(n_chunks // 2):
        rows0 = slice(2 * cp * CHUNK, (2 * cp + 1) * CHUNK)
        rows1 = slice((2 * cp + 1) * CHUNK, (2 * cp + 2) * CHUNK)
        parts0, parts1 = [], []
        for p in range(N_GMLP_GROUPS // 2):
            cols = slice(p * LANES, (p + 1) * LANES)
            r = _dot(w_pairs[p], jnp.concatenate([vn[rows0, cols], vn[rows1, cols]], axis=1))
            parts0.append(jnp.where(lane < HEAD_DIM, r[:CHUNK, :LANES], r[CHUNK:, :LANES]))
            parts1.append(jnp.where(lane < HEAD_DIM, r[:CHUNK, LANES:], r[CHUNK:, LANES:]))
        oa_ref[0, rows0, :] = (u[rows0] * (jnp.concatenate(parts0, axis=1) + bs_ref[...])).astype(BF16)
        oa_ref[0, rows1, :] = (u[rows1] * (jnp.concatenate(parts1, axis=1) + bs_ref[...])).astype(BF16)

    for p in range(N_MEM_HEADS // 2):
        vmp = kv[:, D_MEMQ + p * LANES:D_MEMQ + (p + 1) * LANES]
        outs = []
        for e in range(2):
            s = mem_scores[2 * p + e]
            pe = jnp.exp(s - jnp.max(s, axis=-1, keepdims=True))
            den = jnp.sum(pe, axis=-1, keepdims=True)
            outs.append(_dot(pe.astype(BF16), vmp) / den)
        oc_ref[0, :, p * LANES:(p + 1) * LANES] = jnp.where(lo_half_t, outs[0], outs[1]).astype(BF16)

    extras = _dot(jnp.concatenate(_split_bf16(c_all), axis=1), place_ref[...])
    neg_one = jnp.full((tm, LANES), -1.0, F32)
    one = jnp.ones((tm, LANES), F32)
    zero = jnp.zeros((tm, LANES), F32)
    q_fill_even = jnp.where((lane_t >= HEAD_DIM) & (lane_t < HEAD_DIM + N_SPLIT), neg_one, zero)
    q_fill_odd = jnp.where(lane_t < N_SPLIT, neg_one, zero)
    v_fill_even = jnp.where(lane_t == HEAD_DIM, one, zero)
    v_fill_odd = jnp.where(lane_t == 0, one, zero)
    for p in range(N_FOX_HEADS // 2):
        cols = slice(p * LANES, (p + 1) * LANES)
        zq = zf[:, cols] * QK_SCALE
        zk = zf[:, D_FOX + p * LANES:D_FOX + (p + 1) * LANES]
        zv = zf[:, 2 * D_FOX + p * LANES:2 * D_FOX + (p + 1) * LANES]
        ex = extras[:, cols]
        q_ref[0, 2 * p] = jnp.where(lo_half_t, zq, q_fill_even).astype(BF16)
        q_ref[0, 2 * p + 1] = jnp.where(lo_half_t, q_fill_odd, zq).astype(BF16)
        k_ref[0, 2 * p] = jnp.where(lo_half_t, zk, ex).astype(BF16)
        k_ref[0, 2 * p + 1] = jnp.where(lo_half_t, ex, zk).astype(BF16)
        v_ref[0, 2 * p] = jnp.where(lo_half_t, zv, v_fill_even).T.astype(BF16)
        v_ref[0, 2 * p + 1] = jnp.where(lo_half_t, v_fill_odd, zv).T.astype(BF16)


def _placement_matrix():
    m = np.zeros((N_SPLIT * LANES, D_FOX), np.float32)
    for h in range(N_FOX_HEADS):
        base = (h // 2) * LANES + (HEAD_DIM if h % 2 == 0 else 0)
        for i in range(N_SPLIT):
            m[i * LANES + h, base + i] = 1.0
    return m


def _group_mean_matrix():
    g = np.arange(D_GMLP) // HEAD_DIM
    return (g[:, None] == g[None, :]).astype(np.float32) / HEAD_DIM


def _mix_proj(x, layer, kv_all, p):
    b, s, d = x.shape
    tm = min(TM_PROJ, s)
    assert s % tm == 0 and tm % CHUNK == 0
    m = kv_all.shape[2]
    const2 = lambda shape: pl.BlockSpec(shape, lambda i, j: (0, 0))
    head_spec = pl.BlockSpec((1, N_FOX_HEADS, tm, LANES), lambda i, j: (i, 0, j, 0))
    head_shape = jax.ShapeDtypeStruct((b, N_FOX_HEADS, s, LANES), BF16)
    return pl.pallas_call(
        functools.partial(_mix_proj_kernel, tm=tm),
        grid=(b, s // tm),
        in_specs=[
            pl.BlockSpec((1, tm, d), lambda i, j: (i, j, 0)),
            const2((1, d)),
            const2((d, W_ALL)),
            const2((1, LANES)),
            const2((1, D_GMLP)),
            pl.BlockSpec((N_GMLP_GROUPS, CHUNK, CHUNK), lambda i, j: (0, 0, 0)),
            const2((CHUNK, D_GMLP)),
            pl.BlockSpec((1, 1, m, 2 * D_MEMQ), lambda i, j: (layer, i, 0, 0)),
            const2((D_GMLP, D_GMLP)),
            const2((N_SPLIT * LANES, D_FOX)),
        ],
        out_specs=[
            pl.BlockSpec((1, tm, D_GMLP), lambda i, j: (i, j, 0)),
            pl.BlockSpec((1, tm, D_MEMQ), lambda i, j: (i, j, 0)),
            head_spec, head_spec,
            pl.BlockSpec((1, N_FOX_HEADS, LANES, tm), lambda i, j: (i, 0, 0, j)),
        ],
        out_shape=[
            jax.ShapeDtypeStruct((b, s, D_GMLP), BF16),
            jax.ShapeDtypeStruct((b, s, D_MEMQ), BF16),
            head_shape, head_shape,
            jax.ShapeDtypeStruct((b, N_FOX_HEADS, LANES, s), BF16),
        ],
        scratch_shapes=[pltpu.VMEM((1, LANES), F32)],
        compiler_params=pltpu.CompilerParams(
            dimension_semantics=("arbitrary", "arbitrary"), vmem_limit_bytes=VMEM_LIMIT),
        name="mix_proj",
    )(x, p["g_pre"], p["w_all"], p["b_forget"], p["g_v"], p["w_s"], p["b_s"], kv_all, p["gmean"], p["place"])


def _fox_attn_kernel(q_ref, k_ref, vt_ref, o_ref, sa_ref, sb_ref, m_ref, acc_ref, *, tq, tk):
    assert tq == tk
    hps = q_ref.shape[1]
    qi = pl.program_id(2)
    key_idx = lax.broadcasted_iota(jnp.int32, (tk, tq), 0)
    qry_idx = lax.broadcasted_iota(jnp.int32, (tk, tq), 1)
    sub = lax.broadcasted_iota(jnp.int32, (LANES, tq), 0)

    def scores(j, dst_ref):
        start = pl.multiple_of(j * tk, tk)
        for hh in range(hps):
            dst_ref[hh] = _dot_nt(k_ref[0, hh, pl.ds(start, tk), :], q_ref[0, hh])

    def consume(j, src_ref, masked):
        start = pl.multiple_of(j * tk, tk)
        for hh in range(hps):
            st = src_ref[hh]
            if masked:
                st = jnp.where(key_idx <= qry_idx, st, NEG_INF)
            m = m_ref[hh]
            m_new = jnp.maximum(m, jnp.max(st, axis=0, keepdims=True))
            alpha = jnp.exp(m - m_new)
            pt = jnp.exp(st - m_new).astype(BF16)
            acc_ref[hh] = alpha * acc_ref[hh] + _dot(vt_ref[0, hh, :, pl.ds(start, tk)], pt)
            m_ref[hh] = m_new

    m_ref[...] = jnp.full(m_ref.shape, NEG_INF, F32)
    acc_ref[...] = jnp.zeros(acc_ref.shape, F32)

    scores(0, sa_ref)

    def two_blocks(p, _):
        scores(2 * p + 1, sb_ref)
        consume(2 * p, sa_ref, False)
        scores(2 * p + 2, sa_ref)
        consume(2 * p + 1, sb_ref, False)
        return 0

    lax.fori_loop(0, lax.shift_right_logical(qi, 1), two_blocks, 0)

    @pl.when(lax.rem(qi, 2) == 0)
    def _():
        consume(qi, sa_ref, True)

    @pl.when(lax.rem(qi, 2) == 1)
    def _():
        scores(qi, sb_ref)
        consume(qi - 1, sa_ref, False)
        consume(qi, sb_ref, True)

    for p in range(hps // 2):
        outs = []
        for e in range(2):
            acc = acc_ref[2 * p + e]
            den_row = HEAD_DIM if e == 0 else 0
            outs.append(acc / acc[den_row:den_row + 1, :])
        pair_t = jnp.where(sub < HEAD_DIM, outs[0], outs[1])
        o_ref[0, :, p * LANES:(p + 1) * LANES] = pair_t.T.astype(BF16)


def _fox_attn(q, k, v):
    b, nh, s, _ = q.shape
    tq = min(TQ, s)
    tk = min(TK, tq)
    hps = HEADS_PER_STEP
    assert s % tq == 0 and tq % tk == 0 and nh % hps == 0 and hps % 2 == 0
    q_spec = pl.BlockSpec((1, hps, tq, LANES), lambda i, p, j: (i, p, j, 0))
    k_spec = pl.BlockSpec((1, hps, s, LANES), lambda i, p, j: (i, p, 0, 0))
    vt_spec = pl.BlockSpec((1, hps, LANES, s), lambda i, p, j: (i, p, 0, 0))
    return pl.pallas_call(
        functools.partial(_fox_attn_kernel, tq=tq, tk=tk),
        grid=(b, nh // hps, s // tq),
        in_specs=[q_spec, k_spec, vt_spec],
        out_specs=pl.BlockSpec((1, tq, hps * HEAD_DIM), lambda i, p, j: (i, j, p)),
        out_shape=jax.ShapeDtypeStruct((b, s, nh * HEAD_DIM), BF16),
        scratch_shapes=[
            pltpu.VMEM((hps, tk, tq), F32), pltpu.VMEM((hps, tk, tq), F32),
            pltpu.VMEM((hps, 1, tq), F32), pltpu.VMEM((hps, LANES, tq), F32),
        ],
        compiler_params=pltpu.CompilerParams(
            dimension_semantics=("arbitrary", "arbitrary", "arbitrary"), vmem_limit_bytes=VMEM_LIMIT),
        name="fox_attn",
    )(q, k, v)


def _out_ffn_kernel(x_ref, a_ref, b_ref, c_ref, woa_ref, wob_ref, woc_ref, g1_ref, g2_ref, g3_ref,
                    w1_ref, w2_ref, o_ref, *, ff_chunk):
    y = _dot(a_ref[...], woa_ref[...]) + _dot(b_ref[...], wob_ref[...]) + _dot(c_ref[...], woc_ref[...])
    x1 = x_ref[...] + _rmsnorm(y, g1_ref[...])
    h = _rmsnorm(x1, g2_ref[...]).astype(BF16)
    d_ff = w1_ref.shape[1]
    f = None
    for c in range(d_ff // ff_chunk):
        cols = slice(c * ff_chunk, (c + 1) * ff_chunk)
        a1 = jnp.maximum(_dot(h, w1_ref[:, cols]), 0.0)
        part = _dot((a1 * a1).astype(BF16), w2_ref[cols, :])
        f = part if f is None else f + part
    o_ref[...] = x1 + _rmsnorm(f, g3_ref[...])


def _out_ffn(x, a, b_, c, p):
    t, d = x.shape
    tm = min(TM_FFN, t)
    assert t % tm == 0
    d_ff = p["w_ff1"].shape[1]
    ff_chunk = min(FF_CHUNK, d_ff)
    row = lambda w: pl.BlockSpec((tm, w), lambda i: (i, 0))
    const = lambda shape: pl.BlockSpec(shape, lambda i: (0, 0), pipeline_mode=pl.Buffered(1))
    return pl.pallas_call(
        functools.partial(_out_ffn_kernel, ff_chunk=ff_chunk),
        grid=(t // tm,),
        in_specs=[
            row(d), row(D_GMLP), row(D_FOX), row(D_MEMQ),
            const((D_GMLP, d)), const((D_FOX, d)), const((D_MEMQ, d)),
            const((1, d)), const((1, d)), const((1, d)),
            const((d, d_ff)), const((d_ff, d)),
        ],
        out_specs=row(d),
        out_shape=jax.ShapeDtypeStruct((t, d), F32),
        compiler_params=pltpu.CompilerParams(
            dimension_semantics=("arbitrary",), vmem_limit_bytes=VMEM_LIMIT),
        name="out_ffn",
    )(x, a, b_, c, p["wo_a"], p["wo_b"], p["wo_c"], p["g_post_mix"], p["g_pre_ffn"], p["g_post_ffn"],
      p["w_ff1"], p["w_ff2"])


def _layer_params(l, norm_pre_mix, norm_post_mix, norm_pre_ffn, norm_post_ffn, w_in, b_forget,
                  gmlp_v_norm, gmlp_w_s, gmlp_b_s, w_out, w_ff1, w_ff2):
    d = w_in.shape[1]
    w = w_in[l]
    i_fx = 2 * D_GMLP
    i_gate = i_fx + 3 * D_FOX
    i_mq = i_gate + N_FOX_HEADS
    pad = LANES - N_FOX_HEADS
    wo = w_out[l].astype(BF16)
    w_all = jnp.concatenate(
        [w[:, :i_fx], w[:, i_mq:], jnp.pad(w[:, i_gate:i_mq], ((0, 0), (0, pad))), w[:, i_fx:i_gate]], axis=1)
    return {
        "g_pre": norm_pre_mix[l].reshape(1, d),
        "w_all": w_all.astype(BF16),
        "b_forget": jnp.pad(b_forget[l], (0, pad)).reshape(1, LANES),
        "g_v": gmlp_v_norm[l].reshape(1, D_GMLP),
        "w_s": gmlp_w_s[l],
        "b_s": jnp.repeat(gmlp_b_s[l].T, HEAD_DIM, axis=1),
        "gmean": jnp.asarray(_group_mean_matrix(), BF16),
        "place": jnp.asarray(_placement_matrix(), BF16),
        "wo_a": wo[:D_GMLP],
        "wo_b": wo[D_GMLP:D_GMLP + D_FOX],
        "wo_c": wo[D_GMLP + D_FOX:],
        "g_post_mix": norm_post_mix[l].reshape(1, d),
        "g_pre_ffn": norm_pre_ffn[l].reshape(1, d),
        "g_post_ffn": norm_post_ffn[l].reshape(1, d),
        "w_ff1": w_ff1[l].astype(BF16),
        "w_ff2": w_ff2[l].astype(BF16),
    }


def kernel(x, mem, norm_pre_mix, norm_post_mix, norm_pre_ffn, norm_post_ffn, norm_mem, w_in, b_forget,
           gmlp_v_norm, gmlp_w_s, gmlp_b_s, w_mem_kv, w_out, w_ff1, w_ff2):
    b, s, d = x.shape
    depth = w_in.shape[0]
    kv_all = _memkv(mem, norm_mem, w_mem_kv.astype(BF16))
    for l in range(depth):
        p = _layer_params(l, norm_pre_mix, norm_post_mix, norm_pre_ffn, norm_post_ffn, w_in, b_forget,
                          gmlp_v_norm, gmlp_w_s, gmlp_b_s, w_out, w_ff1, w_ff2)
        out_a, out_c, q, k, v = _mix_proj(x, l, kv_all, p)
        out_b = _fox_attn(q, k, v)
        x = _out_ffn(x.reshape(b * s, d), out_a.reshape(b * s, D_GMLP), out_b.reshape(b * s, D_FOX),
                     out_c.reshape(b * s, D_MEMQ), p).reshape(b, s, d)
    return x
```

```python
import functools
import math

import numpy as np
import jax
import jax.numpy as jnp
from jax import lax
from jax.experimental import pallas as pl
from jax.experimental.pallas import tpu as pltpu

F32 = jnp.float32
BF16 = jnp.bfloat16

HEAD_DIM = 64
LANES = 128
N_GMLP_GROUPS = 6
N_FOX_HEADS = 6
N_MEM_HEADS = 4
D_GMLP = N_GMLP_GROUPS * HEAD_DIM
D_FOX = N_FOX_HEADS * HEAD_DIM
D_MEMQ = N_MEM_HEADS * HEAD_DIM
CHUNK = 128
RMS_EPS = 1e-6
NEG_INF = -1e30
QK_SCALE = HEAD_DIM ** -0.5
LOG2E = math.log2(math.e)
N_SPLIT = 3
O_MQ = 2 * D_GMLP
O_GATE = O_MQ + D_MEMQ
O_FX = O_GATE + LANES
W_ALL = O_FX + 3 * D_FOX

V7X_VMEM_BYTES = 64 * 1024 * 1024
VMEM_LIMIT = V7X_VMEM_BYTES - 8 * 1024 * 1024

TM_PROJ = 512
TQ = 512
HEADS_PER_STEP = 2
TM_FFN = 512
FF_CHUNK = 1024


def _dot(a, b):
    return jnp.dot(a, b, preferred_element_type=F32)


def _dot_nt(a, b):
    return lax.dot_general(a, b, (((1,), (1,)), ((), ())), preferred_element_type=F32)


def _rmsnorm(x, g):
    return x * lax.rsqrt(jnp.mean(x * x, axis=-1, keepdims=True) + RMS_EPS) * g


def _gelu_tanh(x):
    c = math.sqrt(2.0 / math.pi)
    return 0.5 * x * (1.0 + jnp.tanh(c * (x + 0.044715 * (x * x * x))))


def _log_sigmoid(x):
    return -(jnp.maximum(-x, 0.0) + jnp.log1p(jnp.exp(-jnp.abs(x))))


def _split_bf16(x):
    pieces = []
    r = x
    for _ in range(N_SPLIT):
        p = r.astype(BF16)
        pieces.append(p)
        r = r - p.astype(F32)
    return pieces


def _memkv_kernel(mem_ref, g_ref, w_ref, o_ref):
    mn = _rmsnorm(mem_ref[0], g_ref[0]).astype(BF16)
    o_ref[0, 0] = _dot(mn, w_ref[0]).astype(BF16)


def _memkv(mem, norm_mem, w_kv_bf16):
    depth = w_kv_bf16.shape[0]
    b, m, d = mem.shape
    n = w_kv_bf16.shape[-1]
    return pl.pallas_call(
        _memkv_kernel,
        grid=(depth, b),
        in_specs=[
            pl.BlockSpec((1, m, d), lambda l, i: (i, 0, 0)),
            pl.BlockSpec((1, 1, d), lambda l, i: (l, 0, 0)),
            pl.BlockSpec((1, d, n), lambda l, i: (l, 0, 0)),
        ],
        out_specs=pl.BlockSpec((1, 1, m, n), lambda l, i: (l, i, 0, 0)),
        out_shape=jax.ShapeDtypeStruct((depth, b, m, n), BF16),
        compiler_params=pltpu.CompilerParams(dimension_semantics=("arbitrary", "arbitrary")),
        name="mem_kv",
    )(mem, norm_mem.reshape(depth, 1, d), w_kv_bf16)


def _mix_proj_kernel(x_ref, gpre_ref, wall_ref, bf_ref, gv_ref, ws_ref, bs_ref, kv_ref, gmean_ref, place_ref,
                     oa_ref, oc_ref, q_ref, k_ref, v_ref, carry_ref, *, tm):
    n_chunks = tm // CHUNK
    assert n_chunks % 2 == 0

    @pl.when(pl.program_id(1) == 0)
    def _():
        carry_ref[...] = jnp.zeros_like(carry_ref)

    lane = lax.broadcasted_iota(jnp.int32, (CHUNK, LANES), 1)
    lane_t = lax.broadcasted_iota(jnp.int32, (tm, LANES), 1)
    lo_half_t = lane_t < HEAD_DIM
    row = lax.broadcasted_iota(jnp.int32, (CHUNK, CHUNK), 0)
    col = lax.broadcasted_iota(jnp.int32, (CHUNK, CHUNK), 1)
    causal = row >= col

    h = _rmsnorm(x_ref[0], gpre_ref[...]).astype(BF16)
    z = _dot(h, wall_ref[...])
    zg = z[:, :O_MQ]
    zm = z[:, O_MQ:O_GATE] * QK_SCALE
    zgate = z[:, O_GATE:O_FX]
    zf = z[:, O_FX:]

    u = _gelu_tanh(zg[:, :D_GMLP])
    v = _gelu_tanh(zg[:, D_GMLP:])
    ms = _dot((v * v).astype(BF16), gmean_ref[...])

    kv = kv_ref[0, 0]
    mem_scores = []
    for hd in range(N_MEM_HEADS):
        p, e = divmod(hd, 2)
        keep = lo_half_t if e == 0 else jnp.logical_not(lo_half_t)
        qh = jnp.where(keep, zm[:, p * LANES:(p + 1) * LANES], 0.0).astype(BF16)
        mem_scores.append(_dot_nt(qh, kv[:, p * LANES:(p + 1) * LANES]))

    log_f = _log_sigmoid(zgate + bf_ref[...])
    tril = jnp.where(causal, 1.0, 0.0).astype(BF16)
    carry = carry_ref[...]
    cums = []
    for c in range(n_chunks):
        pieces = _split_bf16(log_f[c * CHUNK:(c + 1) * CHUNK])
        cum3 = _dot(tril, jnp.concatenate(pieces, axis=1))
        cum = carry
        for i in range(N_SPLIT):
            cum = cum + cum3[:, i * LANES:(i + 1) * LANES]
        carry = cum[CHUNK - 1:CHUNK, :]
        cums.append(cum)
    carry_ref[...] = carry
    c_all = jnp.concatenate(cums, axis=0)

    vn = (v * lax.rsqrt(ms + RMS_EPS) * gv_ref[...]).astype(BF16)
    w_pairs = []
    for p in range(N_GMLP_GROUPS // 2):
        w0 = jnp.where(causal, ws_ref[2 * p], 0.0).astype(BF16)
        w1 = jnp.where(causal, ws_ref[2 * p + 1], 0.0).astype(BF16)
        w_pairs.append(jnp.concatenate([w0, w1], axis=0))
    for cp in range(n_chunks // 2):
        rows0 = slice(2 * cp * CHUNK, (2 * cp + 1) * CHUNK)
        rows1 = slice((2 * cp + 1) * CHUNK, (2 * cp + 2) * CHUNK)
        parts0, parts1 = [], []
        for p in range(N_GMLP_GROUPS // 2):
            cols = slice(p * LANES, (p + 1) * LANES)
            r = _dot(w_pairs[p], jnp.concatenate([vn[rows0, cols], vn[rows1, cols]], axis=1))
            parts0.append(jnp.where(lane < HEAD_DIM, r[:CHUNK, :LANES], r[CHUNK:, :LANES]))
            parts1.append(jnp.where(lane < HEAD_DIM, r[:CHUNK, LANES:], r[CHUNK:, LANES:]))
        oa_ref[0, rows0, :] = (u[rows0] * (jnp.concatenate(parts0, axis=1) + bs_ref[...])).astype(BF16)
        oa_ref[0, rows1, :] = (u[rows1] * (jnp.concatenate(parts1, axis=1) + bs_ref[...])).astype(BF16)

    for p in range(N_MEM_HEADS // 2):
        vmp = kv[:, D_MEMQ + p * LANES:D_MEMQ + (p + 1) * LANES]
        outs = []
        for e in range(2):
            s = mem_scores[2 * p + e]
            pe = jnp.exp(s - jnp.max(s, axis=-1, keepdims=True))
            den = jnp.sum(pe, axis=-1, keepdims=True)
            outs.append(_dot(pe.astype(BF16), vmp) / den)
        oc_ref[0, :, p * LANES:(p + 1) * LANES] = jnp.where(lo_half_t, outs[0], outs[1]).astype(BF16)

    extras = _dot(jnp.concatenate(_split_bf16(c_all * LOG2E), axis=1), place_ref[...])
    neg_one = jnp.full((tm, LANES), -1.0, F32)
    one = jnp.ones((tm, LANES), F32)
    zero = jnp.zeros((tm, LANES), F32)
    q_fill_even = jnp.where((lane_t >= HEAD_DIM) & (lane_t < HEAD_DIM + N_SPLIT), neg_one, zero)
    q_fill_odd = jnp.where(lane_t < N_SPLIT, neg_one, zero)
    v_fill_even = jnp.where(lane_t == HEAD_DIM, one, zero)
    v_fill_odd = jnp.where(lane_t == 0, one, zero)
    for p in range(N_FOX_HEADS // 2):
        cols = slice(p * LANES, (p + 1) * LANES)
        zq = zf[:, cols] * (QK_SCALE * LOG2E)
        zk = zf[:, D_FOX + p * LANES:D_FOX + (p + 1) * LANES]
        zv = zf[:, 2 * D_FOX + p * LANES:2 * D_FOX + (p + 1) * LANES]
        ex = extras[:, cols]
        q_ref[0, 2 * p] = jnp.where(lo_half_t, zq, q_fill_even).astype(BF16)
        q_ref[0, 2 * p + 1] = jnp.where(lo_half_t, q_fill_odd, zq).astype(BF16)
        k_ref[0, 2 * p] = jnp.where(lo_half_t, zk, ex).astype(BF16)
        k_ref[0, 2 * p + 1] = jnp.where(lo_half_t, ex, zk).astype(BF16)
        v_ref[0, 2 * p] = jnp.where(lo_half_t, zv, v_fill_even).T.astype(BF16)
        v_ref[0, 2 * p + 1] = jnp.where(lo_half_t, v_fill_odd, zv).T.astype(BF16)


def _placement_matrix():
    m = np.zeros((N_SPLIT * LANES, D_FOX), np.float32)
    for h in range(N_FOX_HEADS):
        base = (h // 2) * LANES + (HEAD_DIM if h % 2 == 0 else 0)
        for i in range(N_SPLIT):
            m[i * LANES + h, base + i] = 1.0
    return m


def _group_mean_matrix():
    g = np.arange(D_GMLP) // HEAD_DIM
    return (g[:, None] == g[None, :]).astype(np.float32) / HEAD_DIM


def _mix_proj(x, layer, kv_all, p):
    b, s, d = x.shape
    tm = min(TM_PROJ, s)
    assert s % tm == 0 and tm % CHUNK == 0
    m = kv_all.shape[2]
    const2 = lambda shape: pl.BlockSpec(shape, lambda i, j: (0, 0))
    head_spec = pl.BlockSpec((1, N_FOX_HEADS, tm, LANES), lambda i, j: (i, 0, j, 0))
    head_shape = jax.ShapeDtypeStruct((b, N_FOX_HEADS, s, LANES), BF16)
    return pl.pallas_call(
        functools.partial(_mix_proj_kernel, tm=tm),
        grid=(b, s // tm),
        in_specs=[
            pl.BlockSpec((1, tm, d), lambda i, j: (i, j, 0)),
            const2((1, d)),
            const2((d, W_ALL)),
            const2((1, LANES)),
            const2((1, D_GMLP)),
            pl.BlockSpec((N_GMLP_GROUPS, CHUNK, CHUNK), lambda i, j: (0, 0, 0)),
            const2((CHUNK, D_GMLP)),
            pl.BlockSpec((1, 1, m, 2 * D_MEMQ), lambda i, j: (layer, i, 0, 0)),
            const2((D_GMLP, D_GMLP)),
            const2((N_SPLIT * LANES, D_FOX)),
        ],
        out_specs=[
            pl.BlockSpec((1, tm, D_GMLP), lambda i, j: (i, j, 0)),
            pl.BlockSpec((1, tm, D_MEMQ), lambda i, j: (i, j, 0)),
            head_spec, head_spec,
            pl.BlockSpec((1, N_FOX_HEADS, LANES, tm), lambda i, j: (i, 0, 0, j)),
        ],
        out_shape=[
            jax.ShapeDtypeStruct((b, s, D_GMLP), BF16),
            jax.ShapeDtypeStruct((b, s, D_MEMQ), BF16),
            head_shape, head_shape,
            jax.ShapeDtypeStruct((b, N_FOX_HEADS, LANES, s), BF16),
        ],
        scratch_shapes=[pltpu.VMEM((1, LANES), F32)],
        compiler_params=pltpu.CompilerParams(
            dimension_semantics=("arbitrary", "arbitrary"), vmem_limit_bytes=VMEM_LIMIT),
        name="mix_proj",
    )(x, p["g_pre"], p["w_all"], p["b_forget"], p["g_v"], p["w_s"], p["b_s"], kv_all, p["gmean"], p["place"])


def _fox_attn_kernel(q_ref, k_ref, vt_ref, o_ref, s0_ref, s1_ref, m_ref, acc_ref, *, tq):
    hps = q_ref.shape[1]
    n_q = q_ref.shape[2] // tq
    key_idx = lax.broadcasted_iota(jnp.int32, (tq, tq), 0)
    qry_idx = lax.broadcasted_iota(jnp.int32, (tq, tq), 1)
    sub = lax.broadcasted_iota(jnp.int32, (LANES, tq), 0)
    bufs = (s0_ref, s1_ref)

    def scores(i, j, dst_ref):
        start = pl.multiple_of(j * tq, tq)
        for hh in range(hps):
            dst_ref[hh] = _dot_nt(k_ref[0, hh, pl.ds(start, tq), :], q_ref[0, hh, i * tq:(i + 1) * tq, :])

    def consume(j, src_ref, masked):
        start = pl.multiple_of(j * tq, tq)
        for hh in range(hps):
            st = src_ref[hh]
            if masked:
                st = jnp.where(key_idx <= qry_idx, st, NEG_INF)
            m = m_ref[hh]
            m_new = jnp.maximum(m, jnp.max(st, axis=0, keepdims=True))
            alpha = jnp.exp2(m - m_new)
            pt = jnp.exp2(st - m_new).astype(BF16)
            acc_ref[hh] = alpha * acc_ref[hh] + _dot(vt_ref[0, hh, :, pl.ds(start, tq)], pt)
            m_ref[hh] = m_new

    def finish_tile(i):
        for p in range(hps // 2):
            outs = []
            for e in range(2):
                acc = acc_ref[2 * p + e]
                den_row = HEAD_DIM if e == 0 else 0
                outs.append(acc / acc[den_row:den_row + 1, :])
            pair_t = jnp.where(sub < HEAD_DIM, outs[0], outs[1])
            o_ref[0, i * tq:(i + 1) * tq, p * LANES:(p + 1) * LANES] = pair_t.T.astype(BF16)
        m_ref[...] = jnp.full(m_ref.shape, NEG_INF, F32)
        acc_ref[...] = jnp.zeros(acc_ref.shape, F32)

    m_ref[...] = jnp.full(m_ref.shape, NEG_INF, F32)
    acc_ref[...] = jnp.zeros(acc_ref.shape, F32)
    cur = 0
    scores(0, 0, bufs[0])
    for i in range(n_q):
        x_ref, y_ref = bufs[cur], bufs[1 - cur]

        def two_blocks(p, _, i=i, x_ref=x_ref, y_ref=y_ref):
            scores(i, 2 * p + 1, y_ref)
            consume(2 * p, x_ref, False)
            scores(i, 2 * p + 2, x_ref)
            consume(2 * p + 1, y_ref, False)
            return 0

        if i // 2 > 0:
            lax.fori_loop(0, i // 2, two_blocks, 0)
        if i % 2 == 0:
            if i + 1 < n_q:
                scores(i + 1, 0, y_ref)
            consume(i, x_ref, True)
            cur = 1 - cur
        else:
            scores(i, i, y_ref)
            consume(i - 1, x_ref, False)
            if i + 1 < n_q:
                scores(i + 1, 0, x_ref)
            consume(i, y_ref, True)
        finish_tile(i)


def _fox_attn(q, k, v):
    b, nh, s, _ = q.shape
    tq = min(TQ, s)
    hps = HEADS_PER_STEP
    assert s % tq == 0 and nh % hps == 0 and hps % 2 == 0
    qk_spec = pl.BlockSpec((1, hps, s, LANES), lambda i, p: (i, p, 0, 0))
    vt_spec = pl.BlockSpec((1, hps, LANES, s), lambda i, p: (i, p, 0, 0))
    return pl.pallas_call(
        functools.partial(_fox_attn_kernel, tq=tq),
        grid=(b, nh // hps),
        in_specs=[qk_spec, qk_spec, vt_spec],
        out_specs=pl.BlockSpec((1, s, hps * HEAD_DIM), lambda i, p: (i, 0, p)),
        out_shape=jax.ShapeDtypeStruct((b, s, nh * HEAD_DIM), BF16),
        scratch_shapes=[
            pltpu.VMEM((hps, tq, tq), F32), pltpu.VMEM((hps, tq, tq), F32),
            pltpu.VMEM((hps, 1, tq), F32), pltpu.VMEM((hps, LANES, tq), F32),
        ],
        compiler_params=pltpu.CompilerParams(
            dimension_semantics=("arbitrary", "arbitrary"), vmem_limit_bytes=VMEM_LIMIT),
        name="fox_attn",
    )(q, k, v)


def _out_ffn_kernel(x_ref, a_ref, b_ref, c_ref, woa_ref, wob_ref, woc_ref, g1_ref, g2_ref, g3_ref,
                    w1_ref, w2_ref, o_ref, *, ff_chunk):
    y = _dot(a_ref[...], woa_ref[...]) + _dot(b_ref[...], wob_ref[...]) + _dot(c_ref[...], woc_ref[...])
    x1 = x_ref[...] + _rmsnorm(y, g1_ref[...])
    h = _rmsnorm(x1, g2_ref[...]).astype(BF16)
    d_ff = w1_ref.shape[1]
    f = None
    for c in range(d_ff // ff_chunk):
        cols = slice(c * ff_chunk, (c + 1) * ff_chunk)
        a1 = jnp.maximum(_dot(h, w1_ref[:, cols]), 0.0)
        part = _dot((a1 * a1).astype(BF16), w2_ref[cols, :])
        f = part if f is None else f + part
    o_ref[...] = x1 + _rmsnorm(f, g3_ref[...])


def _out_ffn(x, a, b_, c, p):
    t, d = x.shape
    tm = min(TM_FFN, t)
    assert t % tm == 0
    d_ff = p["w_ff1"].shape[1]
    ff_chunk = min(FF_CHUNK, d_ff)
    row = lambda w: pl.BlockSpec((tm, w), lambda i: (i, 0))
    const = lambda shape: pl.BlockSpec(shape, lambda i: (0, 0), pipeline_mode=pl.Buffered(1))
    return pl.pallas_call(
        functools.partial(_out_ffn_kernel, ff_chunk=ff_chunk),
        grid=(t // tm,),
        in_specs=[
            row(d), row(D_GMLP), row(D_FOX), row(D_MEMQ),
            const((D_GMLP, d)), const((D_FOX, d)), const((D_MEMQ, d)),
            const((1, d)), const((1, d)), const((1, d)),
            const((d, d_ff)), const((d_ff, d)),
        ],
        out_specs=row(d),
        out_shape=jax.ShapeDtypeStruct((t, d), F32),
        compiler_params=pltpu.CompilerParams(
            dimension_semantics=("arbitrary",), vmem_limit_bytes=VMEM_LIMIT),
        name="out_ffn",
    )(x, a, b_, c, p["wo_a"], p["wo_b"], p["wo_c"], p["g_post_mix"], p["g_pre_ffn"], p["g_post_ffn"],
      p["w_ff1"], p["w_ff2"])


def _layer_params(l, norm_pre_mix, norm_post_mix, norm_pre_ffn, norm_post_ffn, w_in, b_forget,
                  gmlp_v_norm, gmlp_w_s, gmlp_b_s, w_out, w_ff1, w_ff2):
    d = w_in.shape[1]
    w = w_in[l]
    i_fx = 2 * D_GMLP
    i_gate = i_fx + 3 * D_FOX
    i_mq = i_gate + N_FOX_HEADS
    pad = LANES - N_FOX_HEADS
    wo = w_out[l].astype(BF16)
    w_all = jnp.concatenate(
        [w[:, :i_fx], w[:, i_mq:], jnp.pad(w[:, i_gate:i_mq], ((0, 0), (0, pad))), w[:, i_fx:i_gate]], axis=1)
    return {
        "g_pre": norm_pre_mix[l].reshape(1, d),
        "w_all": w_all.astype(BF16),
        "b_forget": jnp.pad(b_forget[l], (0, pad)).reshape(1, LANES),
        "g_v": gmlp_v_norm[l].reshape(1, D_GMLP),
        "w_s": gmlp_w_s[l],
        "b_s": jnp.repeat(gmlp_b_s[l].T, HEAD_DIM, axis=1),
        "gmean": jnp.asarray(_group_mean_matrix(), BF16),
        "place": jnp.asarray(_placement_matrix(), BF16),
        "wo_a": wo[:D_GMLP],
        "wo_b": wo[D_GMLP:D_GMLP + D_FOX],
        "wo_c": wo[D_GMLP + D_FOX:],
        "g_post_mix": norm_post_mix[l].reshape(1, d),
        "g_pre_ffn": norm_pre_ffn[l].reshape(1, d),
        "g_post_ffn": norm_post_ffn[l].reshape(1, d),
        "w_ff1": w_ff1[l].astype(BF16),
        "w_ff2": w_ff2[l].astype(BF16),
    }


def kernel(x, mem, norm_pre_mix, norm_post_mix, norm_pre_ffn, norm_post_ffn, norm_mem, w_in, b_forget,
           gmlp_v_norm, gmlp_w_s, gmlp_b_s, w_mem_kv, w_out, w_ff1, w_ff2):
    b, s, d = x.shape
    depth = w_in.shape[0]
    kv_all = _memkv(mem, norm_mem, w_mem_kv.astype(BF16))
    for l in range(depth):
        p = _layer_params(l, norm_pre_mix, norm_post_mix, norm_pre_ffn, norm_post_ffn, w_in, b_forget,
                          gmlp_v_norm, gmlp_w_s, gmlp_b_s, w_out, w_ff1, w_ff2)
        out_a, out_c, q, k, v = _mix_proj(x, l, kv_all, p)
        out_b = _fox_attn(q, k, v)
        x = _out_ffn(x.reshape(b * s, d), out_a.reshape(b * s, D_GMLP), out_b.reshape(b * s, D_FOX),
                     out_c.reshape(b * s, D_MEMQ), p).reshape(b, s, d)
    return x
```

```python
import functools
import math

import numpy as np
import jax
import jax.numpy as jnp
from jax import lax
from jax.experimental import pallas as pl
from jax.experimental.pallas import tpu as pltpu

F32 = jnp.float32
BF16 = jnp.bfloat16

HEAD_DIM = 64
LANES = 128
N_GMLP_GROUPS = 6
N_FOX_HEADS = 6
N_MEM_HEADS = 4
D_GMLP = N_GMLP_GROUPS * HEAD_DIM
D_FOX = N_FOX_HEADS * HEAD_DIM
D_MEMQ = N_MEM_HEADS * HEAD_DIM
CHUNK = 128
RMS_EPS = 1e-6
NEG_INF = -1e30
QK_SCALE = HEAD_DIM ** -0.5
LOG2E = math.log2(math.e)
N_SPLIT = 3
O_MQ = 2 * D_GMLP
O_GATE = O_MQ + D_MEMQ
O_FX = O_GATE + LANES
W_ALL = O_FX + 3 * D_FOX

V7X_VMEM_BYTES = 64 * 1024 * 1024
VMEM_LIMIT = V7X_VMEM_BYTES - 8 * 1024 * 1024

TM_PROJ = 1024
TQ = 512
HEADS_PER_STEP = 2
TM_FFN = 1024
FF_CHUNK = 1024


def _dot(a, b):
    return jnp.dot(a, b, preferred_element_type=F32)


def _dot_nt(a, b):
    return lax.dot_general(a, b, (((1,), (1,)), ((), ())), preferred_element_type=F32)


def _rmsnorm(x, g):
    return x * lax.rsqrt(jnp.mean(x * x, axis=-1, keepdims=True) + RMS_EPS) * g


def _gelu_tanh(x):
    c = math.sqrt(2.0 / math.pi)
    return 0.5 * x * (1.0 + jnp.tanh(c * (x + 0.044715 * (x * x * x))))


def _log_sigmoid(x):
    return -(jnp.maximum(-x, 0.0) + jnp.log1p(jnp.exp(-jnp.abs(x))))


def _split_bf16(x):
    pieces = []
    r = x
    for _ in range(N_SPLIT):
        p = r.astype(BF16)
        pieces.append(p)
        r = r - p.astype(F32)
    return pieces


def _memkv_kernel(mem_ref, g_ref, w_ref, o_ref):
    mn = _rmsnorm(mem_ref[0], g_ref[0]).astype(BF16)
    o_ref[0, 0] = _dot(mn, w_ref[0]).astype(BF16)


def _memkv(mem, norm_mem, w_kv_bf16):
    depth = w_kv_bf16.shape[0]
    b, m, d = mem.shape
    n = w_kv_bf16.shape[-1]
    return pl.pallas_call(
        _memkv_kernel,
        grid=(depth, b),
        in_specs=[
            pl.BlockSpec((1, m, d), lambda l, i: (i, 0, 0)),
            pl.BlockSpec((1, 1, d), lambda l, i: (l, 0, 0)),
            pl.BlockSpec((1, d, n), lambda l, i: (l, 0, 0)),
        ],
        out_specs=pl.BlockSpec((1, 1, m, n), lambda l, i: (l, i, 0, 0)),
        out_shape=jax.ShapeDtypeStruct((depth, b, m, n), BF16),
        compiler_params=pltpu.CompilerParams(dimension_semantics=("arbitrary", "arbitrary")),
        name="mem_kv",
    )(mem, norm_mem.reshape(depth, 1, d), w_kv_bf16)


def _mix_proj_kernel(x_ref, gpre_ref, wall_ref, bf_ref, gv_ref, ws_ref, bs_ref, kv_ref, gmean_ref, place_ref,
                     oa_ref, oc_ref, q_ref, k_ref, v_ref, carry_ref, *, tm):
    n_chunks = tm // CHUNK
    assert n_chunks % 2 == 0

    @pl.when(pl.program_id(1) == 0)
    def _():
        carry_ref[...] = jnp.zeros_like(carry_ref)

    lane = lax.broadcasted_iota(jnp.int32, (CHUNK, LANES), 1)
    lane_t = lax.broadcasted_iota(jnp.int32, (tm, LANES), 1)
    lo_half_t = lane_t < HEAD_DIM
    row = lax.broadcasted_iota(jnp.int32, (CHUNK, CHUNK), 0)
    col = lax.broadcasted_iota(jnp.int32, (CHUNK, CHUNK), 1)
    causal = row >= col

    h = _rmsnorm(x_ref[0], gpre_ref[...]).astype(BF16)
    z = _dot(h, wall_ref[...])
    zg = z[:, :O_MQ]
    zm = z[:, O_MQ:O_GATE] * QK_SCALE
    zgate = z[:, O_GATE:O_FX]
    zf = z[:, O_FX:]

    u = _gelu_tanh(zg[:, :D_GMLP])
    v = _gelu_tanh(zg[:, D_GMLP:])
    ms = _dot((v * v).astype(BF16), gmean_ref[...])

    kv = kv_ref[0, 0]
    mem_scores = []
    for hd in range(N_MEM_HEADS):
        p, e = divmod(hd, 2)
        keep = lo_half_t if e == 0 else jnp.logical_not(lo_half_t)
        qh = jnp.where(keep, zm[:, p * LANES:(p + 1) * LANES], 0.0).astype(BF16)
        mem_scores.append(_dot_nt(qh, kv[:, p * LANES:(p + 1) * LANES]))

    log_f = _log_sigmoid(zgate + bf_ref[...])
    tril = jnp.where(causal, 1.0, 0.0).astype(BF16)
    carry = carry_ref[...]
    cums = []
    for c in range(n_chunks):
        pieces = _split_bf16(log_f[c * CHUNK:(c + 1) * CHUNK])
        cum3 = _dot(tril, jnp.concatenate(pieces, axis=1))
        cum = carry
        for i in range(N_SPLIT):
            cum = cum + cum3[:, i * LANES:(i + 1) * LANES]
        carry = cum[CHUNK - 1:CHUNK, :]
        cums.append(cum)
    carry_ref[...] = carry
    c_all = jnp.concatenate(cums, axis=0)

    vn = (v * lax.rsqrt(ms + RMS_EPS) * gv_ref[...]).astype(BF16)
    w_pairs = []
    for p in range(N_GMLP_GROUPS // 2):
        w0 = jnp.where(causal, ws_ref[2 * p], 0.0).astype(BF16)
        w1 = jnp.where(causal, ws_ref[2 * p + 1], 0.0).astype(BF16)
        w_pairs.append(jnp.concatenate([w0, w1], axis=0))
    for cp in range(n_chunks // 2):
        rows0 = slice(2 * cp * CHUNK, (2 * cp + 1) * CHUNK)
        rows1 = slice((2 * cp + 1) * CHUNK, (2 * cp + 2) * CHUNK)
        parts0, parts1 = [], []
        for p in range(N_GMLP_GROUPS // 2):
            cols = slice(p * LANES, (p + 1) * LANES)
            r = _dot(w_pairs[p], jnp.concatenate([vn[rows0, cols], vn[rows1, cols]], axis=1))
            parts0.append(jnp.where(lane < HEAD_DIM, r[:CHUNK, :LANES], r[CHUNK:, :LANES]))
            parts1.append(jnp.where(lane < HEAD_DIM, r[:CHUNK, LANES:], r[CHUNK:, LANES:]))
        oa_ref[0, rows0, :] = (u[rows0] * (jnp.concatenate(parts0, axis=1) + bs_ref[...])).astype(BF16)
        oa_ref[0, rows1, :] = (u[rows1] * (jnp.concatenate(parts1, axis=1) + bs_ref[...])).astype(BF16)

    for p in range(N_MEM_HEADS // 2):
        vmp = kv[:, D_MEMQ + p * LANES:D_MEMQ + (p + 1) * LANES]
        outs = []
        for e in range(2):
            s = mem_scores[2 * p + e]
            pe = jnp.exp(s - jnp.max(s, axis=-1, keepdims=True))
            den = jnp.sum(pe, axis=-1, keepdims=True)
            outs.append(_dot(pe.astype(BF16), vmp) / den)
        oc_ref[0, :, p * LANES:(p + 1) * LANES] = jnp.where(lo_half_t, outs[0], outs[1]).astype(BF16)

    extras = _dot(jnp.concatenate(_split_bf16(c_all * LOG2E), axis=1), place_ref[...])
    neg_one = jnp.full((tm, LANES), -1.0, F32)
    one = jnp.ones((tm, LANES), F32)
    zero = jnp.zeros((tm, LANES), F32)
    q_fill_even = jnp.where((lane_t >= HEAD_DIM) & (lane_t < HEAD_DIM + N_SPLIT), neg_one, zero)
    q_fill_odd = jnp.where(lane_t < N_SPLIT, neg_one, zero)
    v_fill_even = jnp.where(lane_t == HEAD_DIM, one, zero)
    v_fill_odd = jnp.where(lane_t == 0, one, zero)
    for p in range(N_FOX_HEADS // 2):
        cols = slice(p * LANES, (p + 1) * LANES)
        zq = zf[:, cols] * (QK_SCALE * LOG2E)
        zk = zf[:, D_FOX + p * LANES:D_FOX + (p + 1) * LANES]
        zv = zf[:, 2 * D_FOX + p * LANES:2 * D_FOX + (p + 1) * LANES]
        ex = extras[:, cols]
        q_ref[0, 2 * p] = jnp.where(lo_half_t, zq, q_fill_even).astype(BF16)
        q_ref[0, 2 * p + 1] = jnp.where(lo_half_t, q_fill_odd, zq).astype(BF16)
        k_ref[0, 2 * p] = jnp.where(lo_half_t, zk, ex).astype(BF16)
        k_ref[0, 2 * p + 1] = jnp.where(lo_half_t, ex, zk).astype(BF16)
        v_ref[0, 2 * p] = jnp.where(lo_half_t, zv, v_fill_even).T.astype(BF16)
        v_ref[0, 2 * p + 1] = jnp.where(lo_half_t, v_fill_odd, zv).T.astype(BF16)


def _placement_matrix():
    m = np.zeros((N_SPLIT * LANES, D_FOX), np.float32)
    for h in range(N_FOX_HEADS):
        base = (h // 2) * LANES + (HEAD_DIM if h % 2 == 0 else 0)
        for i in range(N_SPLIT):
            m[i * LANES + h, base + i] = 1.0
    return m


def _group_mean_matrix():
    g = np.arange(D_GMLP) // HEAD_DIM
    return (g[:, None] == g[None, :]).astype(np.float32) / HEAD_DIM


def _mix_proj(x, layer, kv_all, p):
    b, s, d = x.shape
    tm = min(TM_PROJ, s)
    assert s % tm == 0 and tm % CHUNK == 0
    m = kv_all.shape[2]
    const2 = lambda shape: pl.BlockSpec(shape, lambda i, j: (0, 0))
    per_layer = lambda *shape: pl.BlockSpec((None,) + shape, lambda i, j: (layer,) + (0,) * len(shape))
    head_spec = pl.BlockSpec((1, N_FOX_HEADS, tm, LANES), lambda i, j: (i, 0, j, 0))
    head_shape = jax.ShapeDtypeStruct((b, N_FOX_HEADS, s, LANES), BF16)
    return pl.pallas_call(
        functools.partial(_mix_proj_kernel, tm=tm),
        grid=(b, s // tm),
        in_specs=[
            pl.BlockSpec((1, tm, d), lambda i, j: (i, j, 0)),
            per_layer(1, d),
            per_layer(d, W_ALL),
            per_layer(1, LANES),
            per_layer(1, D_GMLP),
            per_layer(N_GMLP_GROUPS, CHUNK, CHUNK),
            per_layer(CHUNK, D_GMLP),
            pl.BlockSpec((1, 1, m, 2 * D_MEMQ), lambda i, j: (layer, i, 0, 0)),
            const2((D_GMLP, D_GMLP)),
            const2((N_SPLIT * LANES, D_FOX)),
        ],
        out_specs=[
            pl.BlockSpec((1, tm, D_GMLP), lambda i, j: (i, j, 0)),
            pl.BlockSpec((1, tm, D_MEMQ), lambda i, j: (i, j, 0)),
            head_spec, head_spec,
            pl.BlockSpec((1, N_FOX_HEADS, LANES, tm), lambda i, j: (i, 0, 0, j)),
        ],
        out_shape=[
            jax.ShapeDtypeStruct((b, s, D_GMLP), BF16),
            jax.ShapeDtypeStruct((b, s, D_MEMQ), BF16),
            head_shape, head_shape,
            jax.ShapeDtypeStruct((b, N_FOX_HEADS, LANES, s), BF16),
        ],
        scratch_shapes=[pltpu.VMEM((1, LANES), F32)],
        compiler_params=pltpu.CompilerParams(
            dimension_semantics=("arbitrary", "arbitrary"), vmem_limit_bytes=VMEM_LIMIT),
        name="mix_proj",
    )(x, p["g_pre_mix"], p["w_all"], p["b_forget"], p["g_v"], p["w_s"], p["b_s"], kv_all, p["gmean"], p["place"])


def _fox_attn_kernel(q_ref, k_ref, vt_ref, o_ref, s0_ref, s1_ref, m_ref, acc_ref, *, tq):
    hps = q_ref.shape[1]
    n_q = q_ref.shape[2] // tq
    key_idx = lax.broadcasted_iota(jnp.int32, (tq, tq), 0)
    qry_idx = lax.broadcasted_iota(jnp.int32, (tq, tq), 1)
    sub = lax.broadcasted_iota(jnp.int32, (LANES, tq), 0)
    bufs = (s0_ref, s1_ref)

    def scores(i, j, dst_ref):
        start = pl.multiple_of(j * tq, tq)
        for hh in range(hps):
            dst_ref[hh] = _dot_nt(k_ref[0, hh, pl.ds(start, tq), :], q_ref[0, hh, i * tq:(i + 1) * tq, :])

    def consume(j, src_ref, masked):
        start = pl.multiple_of(j * tq, tq)
        for hh in range(hps):
            st = src_ref[hh]
            if masked:
                st = jnp.where(key_idx <= qry_idx, st, NEG_INF)
            m = m_ref[hh]
            m_new = jnp.maximum(m, jnp.max(st, axis=0, keepdims=True))
            alpha = jnp.exp2(m - m_new)
            pt = jnp.exp2(st - m_new).astype(BF16)
            acc_ref[hh] = alpha * acc_ref[hh] + _dot(vt_ref[0, hh, :, pl.ds(start, tq)], pt)
            m_ref[hh] = m_new

    def finish_tile(i):
        for p in range(hps // 2):
            outs = []
            for e in range(2):
                acc = acc_ref[2 * p + e]
                den_row = HEAD_DIM if e == 0 else 0
                outs.append(acc / acc[den_row:den_row + 1, :])
            pair_t = jnp.where(sub < HEAD_DIM, outs[0], outs[1])
            o_ref[0, i * tq:(i + 1) * tq, p * LANES:(p + 1) * LANES] = pair_t.T.astype(BF16)
        m_ref[...] = jnp.full(m_ref.shape, NEG_INF, F32)
        acc_ref[...] = jnp.zeros(acc_ref.shape, F32)

    m_ref[...] = jnp.full(m_ref.shape, NEG_INF, F32)
    acc_ref[...] = jnp.zeros(acc_ref.shape, F32)
    cur = 0
    scores(0, 0, bufs[0])
    for i in range(n_q):
        x_ref, y_ref = bufs[cur], bufs[1 - cur]

        def two_blocks(p, _, i=i, x_ref=x_ref, y_ref=y_ref):
            scores(i, 2 * p + 1, y_ref)
            consume(2 * p, x_ref, False)
            scores(i, 2 * p + 2, x_ref)
            consume(2 * p + 1, y_ref, False)
            return 0

        if i // 2 > 0:
            lax.fori_loop(0, i // 2, two_blocks, 0)
        if i % 2 == 0:
            if i + 1 < n_q:
                scores(i + 1, 0, y_ref)
            consume(i, x_ref, True)
            cur = 1 - cur
        else:
            scores(i, i, y_ref)
            consume(i - 1, x_ref, False)
            if i + 1 < n_q:
                scores(i + 1, 0, x_ref)
            consume(i, y_ref, True)
        finish_tile(i)


def _fox_attn(q, k, v):
    b, nh, s, _ = q.shape
    tq = min(TQ, s)
    hps = HEADS_PER_STEP
    assert s % tq == 0 and nh % hps == 0 and hps % 2 == 0
    qk_spec = pl.BlockSpec((1, hps, s, LANES), lambda i, p: (i, p, 0, 0))
    vt_spec = pl.BlockSpec((1, hps, LANES, s), lambda i, p: (i, p, 0, 0))
    return pl.pallas_call(
        functools.partial(_fox_attn_kernel, tq=tq),
        grid=(b, nh // hps),
        in_specs=[qk_spec, qk_spec, vt_spec],
        out_specs=pl.BlockSpec((1, s, hps * HEAD_DIM), lambda i, p: (i, 0, p)),
        out_shape=jax.ShapeDtypeStruct((b, s, nh * HEAD_DIM), BF16),
        scratch_shapes=[
            pltpu.VMEM((hps, tq, tq), F32), pltpu.VMEM((hps, tq, tq), F32),
            pltpu.VMEM((hps, 1, tq), F32), pltpu.VMEM((hps, LANES, tq), F32),
        ],
        compiler_params=pltpu.CompilerParams(
            dimension_semantics=("arbitrary", "arbitrary"), vmem_limit_bytes=VMEM_LIMIT),
        name="fox_attn",
    )(q, k, v)


def _out_ffn_kernel(x_ref, a_ref, b_ref, c_ref, woa_ref, wob_ref, woc_ref, g1_ref, g2_ref, g3_ref,
                    w1_ref, w2_ref, o_ref, *, ff_chunk):
    y = _dot(a_ref[...], woa_ref[...]) + _dot(b_ref[...], wob_ref[...]) + _dot(c_ref[...], woc_ref[...])
    x1 = x_ref[...] + _rmsnorm(y, g1_ref[...])
    h = _rmsnorm(x1, g2_ref[...]).astype(BF16)
    d_ff = w1_ref.shape[1]
    f = None
    for c in range(d_ff // ff_chunk):
        cols = slice(c * ff_chunk, (c + 1) * ff_chunk)
        a1 = jnp.maximum(_dot(h, w1_ref[:, cols]), 0.0)
        part = _dot((a1 * a1).astype(BF16), w2_ref[cols, :])
        f = part if f is None else f + part
    o_ref[...] = x1 + _rmsnorm(f, g3_ref[...])


def _out_ffn(x, a, b_, c, layer, p):
    t, d = x.shape
    tm = min(TM_FFN, t)
    assert t % tm == 0
    d_ff = p["w_ff1"].shape[2]
    ff_chunk = min(FF_CHUNK, d_ff)
    assert D_GMLP == D_FOX and (D_GMLP + D_FOX) % D_MEMQ == 0
    row = lambda w: pl.BlockSpec((tm, w), lambda i: (i, 0))
    per_layer = lambda shape, blk=0: pl.BlockSpec(
        (None,) + shape, lambda i: (layer, blk) + (0,) * (len(shape) - 1), pipeline_mode=pl.Buffered(1))
    return pl.pallas_call(
        functools.partial(_out_ffn_kernel, ff_chunk=ff_chunk),
        grid=(t // tm,),
        in_specs=[
            row(d), row(D_GMLP), row(D_FOX), row(D_MEMQ),
            per_layer((D_GMLP, d), 0), per_layer((D_FOX, d), 1), per_layer((D_MEMQ, d), (D_GMLP + D_FOX) // D_MEMQ),
            per_layer((1, d)), per_layer((1, d)), per_layer((1, d)),
            per_layer((d, d_ff)), per_layer((d_ff, d)),
        ],
        out_specs=row(d),
        out_shape=jax.ShapeDtypeStruct((t, d), F32),
        compiler_params=pltpu.CompilerParams(
            dimension_semantics=("arbitrary",), vmem_limit_bytes=VMEM_LIMIT),
        name="out_ffn",
    )(x, a, b_, c, p["w_out"], p["w_out"], p["w_out"], p["g_post_mix"], p["g_pre_ffn"], p["g_post_ffn"],
      p["w_ff1"], p["w_ff2"])


def _pack_params(norm_pre_mix, norm_post_mix, norm_pre_ffn, norm_post_ffn, w_in, b_forget,
                 gmlp_v_norm, gmlp_w_s, gmlp_b_s, w_out, w_ff1, w_ff2):
    depth, d, _ = w_in.shape
    i_fx = 2 * D_GMLP
    i_gate = i_fx + 3 * D_FOX
    i_mq = i_gate + N_FOX_HEADS
    pad = LANES - N_FOX_HEADS
    w_all = jnp.concatenate(
        [w_in[:, :, :i_fx], w_in[:, :, i_mq:], jnp.pad(w_in[:, :, i_gate:i_mq], ((0, 0), (0, 0), (0, pad))),
         w_in[:, :, i_fx:i_gate]], axis=2)
    return {
        "g_pre_mix": norm_pre_mix.reshape(depth, 1, d),
        "w_all": w_all.astype(BF16),
        "b_forget": jnp.pad(b_forget, ((0, 0), (0, pad))).reshape(depth, 1, LANES),
        "g_v": gmlp_v_norm.reshape(depth, 1, D_GMLP),
        "w_s": gmlp_w_s,
        "b_s": jnp.repeat(jnp.swapaxes(gmlp_b_s, 1, 2), HEAD_DIM, axis=2),
        "gmean": jnp.asarray(_group_mean_matrix(), BF16),
        "place": jnp.asarray(_placement_matrix(), BF16),
        "w_out": w_out.astype(BF16),
        "g_post_mix": norm_post_mix.reshape(depth, 1, d),
        "g_pre_ffn": norm_pre_ffn.reshape(depth, 1, d),
        "g_post_ffn": norm_post_ffn.reshape(depth, 1, d),
        "w_ff1": w_ff1.astype(BF16),
        "w_ff2": w_ff2.astype(BF16),
    }


def kernel(x, mem, norm_pre_mix, norm_post_mix, norm_pre_ffn, norm_post_ffn, norm_mem, w_in, b_forget,
           gmlp_v_norm, gmlp_w_s, gmlp_b_s, w_mem_kv, w_out, w_ff1, w_ff2):
    b, s, d = x.shape
    depth = w_in.shape[0]
    p = _pack_params(norm_pre_mix, norm_post_mix, norm_pre_ffn, norm_post_ffn, w_in, b_forget,
                     gmlp_v_norm, gmlp_w_s, gmlp_b_s, w_out, w_ff1, w_ff2)
    kv_all = _memkv(mem, norm_mem, w_mem_kv.astype(BF16))
    for l in range(depth):
        out_a, out_c, q, k, v = _mix_proj(x, l, kv_all, p)
        out_b = _fox_attn(q, k, v)
        x = _out_ffn(x.reshape(b * s, d), out_a.reshape(b * s, D_GMLP), out_b.reshape(b * s, D_FOX),
                     out_c.reshape(b * s, D_MEMQ), l, p).reshape(b, s, d)
    return x
```

```python
import functools
import math

import numpy as np
import jax
import jax.numpy as jnp
from jax import lax
from jax.experimental import pallas as pl
from jax.experimental.pallas import tpu as pltpu

F32 = jnp.float32
BF16 = jnp.bfloat16

HEAD_DIM = 64
LANES = 128
N_GMLP_GROUPS = 6
N_FOX_HEADS = 6
N_MEM_HEADS = 4
D_GMLP = N_GMLP_GROUPS * HEAD_DIM
D_FOX = N_FOX_HEADS * HEAD_DIM
D_MEMQ = N_MEM_HEADS * HEAD_DIM
CHUNK = 128
RMS_EPS = 1e-6
NEG_INF = -1e30
QK_SCALE = HEAD_DIM ** -0.5
LOG2E = math.log2(math.e)
N_SPLIT = 3
O_FX = 2 * D_GMLP
O_GATE = O_FX + 3 * D_FOX
O_MQ = O_GATE + LANES
W_ALL = O_MQ + D_MEMQ

V7X_VMEM_BYTES = 64 * 1024 * 1024
VMEM_LIMIT = V7X_VMEM_BYTES - 8 * 1024 * 1024

TM_PROJ = 1024
TQ = 512
HEADS_PER_STEP = 2
TM_FFN = 1024
FF_CHUNK = 1024


def _dot(a, b):
    return jnp.dot(a, b, preferred_element_type=F32)


def _dot_nt(a, b):
    return lax.dot_general(a, b, (((1,), (1,)), ((), ())), preferred_element_type=F32)


def _rmsnorm(x, g):
    return x * lax.rsqrt(jnp.mean(x * x, axis=-1, keepdims=True) + RMS_EPS) * g


def _gelu_tanh(x):
    c = math.sqrt(2.0 / math.pi)
    return 0.5 * x * (1.0 + jnp.tanh(c * (x + 0.044715 * (x * x * x))))


def _log_sigmoid(x):
    return -(jnp.maximum(-x, 0.0) + jnp.log1p(jnp.exp(-jnp.abs(x))))


def _split_bf16(x):
    pieces = []
    r = x
    for _ in range(N_SPLIT):
        p = r.astype(BF16)
        pieces.append(p)
        r = r - p.astype(F32)
    return pieces


def _memkv_kernel(mem_ref, g_ref, w_ref, o_ref):
    mn = _rmsnorm(mem_ref[...], g_ref[...]).astype(BF16)
    o_ref[...] = _dot(mn, w_ref[...]).astype(BF16)


def _memkv(mem, norm_mem, w_kv_bf16):
    depth = w_kv_bf16.shape[0]
    b, m, d = mem.shape
    n = w_kv_bf16.shape[-1]
    kv = pl.pallas_call(
        _memkv_kernel,
        grid=(depth,),
        in_specs=[
            pl.BlockSpec((b * m, d), lambda l: (0, 0)),
            pl.BlockSpec((None, 1, d), lambda l: (l, 0, 0)),
            pl.BlockSpec((None, d, n), lambda l: (l, 0, 0)),
        ],
        out_specs=pl.BlockSpec((None, b * m, n), lambda l: (l, 0, 0)),
        out_shape=jax.ShapeDtypeStruct((depth, b * m, n), BF16),
        compiler_params=pltpu.CompilerParams(dimension_semantics=("arbitrary",), vmem_limit_bytes=VMEM_LIMIT),
        name="mem_kv",
    )(mem.reshape(b * m, d), norm_mem.reshape(depth, 1, d), w_kv_bf16)
    return kv.reshape(depth, b, m, n)


def _mix_proj_kernel(x_ref, gpre_ref, wall_ref, bf_ref, gv_ref, ws_ref, bs_ref, kv_ref, gmean_ref, place_ref,
                     oa_ref, oc_ref, q_ref, k_ref, v_ref, carry_ref, *, tm):
    n_chunks = tm // CHUNK
    assert n_chunks % 2 == 0

    @pl.when(pl.program_id(1) == 0)
    def _():
        carry_ref[...] = jnp.zeros_like(carry_ref)

    lane = lax.broadcasted_iota(jnp.int32, (CHUNK, LANES), 1)
    lane_t = lax.broadcasted_iota(jnp.int32, (tm, LANES), 1)
    lo_half_t = lane_t < HEAD_DIM
    row = lax.broadcasted_iota(jnp.int32, (CHUNK, CHUNK), 0)
    col = lax.broadcasted_iota(jnp.int32, (CHUNK, CHUNK), 1)
    causal = row >= col

    h = _rmsnorm(x_ref[0], gpre_ref[...]).astype(BF16)
    z = _dot(h, wall_ref[...])
    zg = z[:, :O_FX]
    zf = z[:, O_FX:O_GATE]
    zgate = z[:, O_GATE:O_MQ]
    zm = z[:, O_MQ:] * QK_SCALE

    u = _gelu_tanh(zg[:, :D_GMLP])
    v = _gelu_tanh(zg[:, D_GMLP:])
    ms = _dot((v * v).astype(BF16), gmean_ref[...])

    kv = kv_ref[0, 0]
    mem_scores = []
    for hd in range(N_MEM_HEADS):
        p, e = divmod(hd, 2)
        keep = lo_half_t if e == 0 else jnp.logical_not(lo_half_t)
        qh = jnp.where(keep, zm[:, p * LANES:(p + 1) * LANES], 0.0).astype(BF16)
        mem_scores.append(_dot_nt(qh, kv[:, p * LANES:(p + 1) * LANES]))

    log_f = _log_sigmoid(zgate + bf_ref[...])
    tril = jnp.where(causal, 1.0, 0.0).astype(BF16)
    carry = carry_ref[...]
    cums = []
    for c in range(n_chunks):
        pieces = _split_bf16(log_f[c * CHUNK:(c + 1) * CHUNK])
        cum3 = _dot(tril, jnp.concatenate(pieces, axis=1))
        cum = carry
        for i in range(N_SPLIT):
            cum = cum + cum3[:, i * LANES:(i + 1) * LANES]
        carry = cum[CHUNK - 1:CHUNK, :]
        cums.append(cum)
    carry_ref[...] = carry
    c_all = jnp.concatenate(cums, axis=0)

    vn = (v * lax.rsqrt(ms + RMS_EPS) * gv_ref[...]).astype(BF16)
    w_pairs = []
    for p in range(N_GMLP_GROUPS // 2):
        w0 = jnp.where(causal, ws_ref[2 * p], 0.0).astype(BF16)
        w1 = jnp.where(causal, ws_ref[2 * p + 1], 0.0).astype(BF16)
        w_pairs.append(jnp.concatenate([w0, w1], axis=0))
    for cp in range(n_chunks // 2):
        rows0 = slice(2 * cp * CHUNK, (2 * cp + 1) * CHUNK)
        rows1 = slice((2 * cp + 1) * CHUNK, (2 * cp + 2) * CHUNK)
        parts0, parts1 = [], []
        for p in range(N_GMLP_GROUPS // 2):
            cols = slice(p * LANES, (p + 1) * LANES)
            r = _dot(w_pairs[p], jnp.concatenate([vn[rows0, cols], vn[rows1, cols]], axis=1))
            parts0.append(jnp.where(lane < HEAD_DIM, r[:CHUNK, :LANES], r[CHUNK:, :LANES]))
            parts1.append(jnp.where(lane < HEAD_DIM, r[:CHUNK, LANES:], r[CHUNK:, LANES:]))
        oa_ref[0, rows0, :] = (u[rows0] * (jnp.concatenate(parts0, axis=1) + bs_ref[...])).astype(BF16)
        oa_ref[0, rows1, :] = (u[rows1] * (jnp.concatenate(parts1, axis=1) + bs_ref[...])).astype(BF16)

    for p in range(N_MEM_HEADS // 2):
        vmp = kv[:, D_MEMQ + p * LANES:D_MEMQ + (p + 1) * LANES]
        outs = []
        for e in range(2):
            s = mem_scores[2 * p + e]
            pe = jnp.exp(s - jnp.max(s, axis=-1, keepdims=True))
            den = jnp.sum(pe, axis=-1, keepdims=True)
            outs.append(_dot(pe.astype(BF16), vmp) / den)
        oc_ref[0, :, p * LANES:(p + 1) * LANES] = jnp.where(lo_half_t, outs[0], outs[1]).astype(BF16)

    extras = _dot(jnp.concatenate(_split_bf16(c_all * LOG2E), axis=1), place_ref[...])
    neg_one = jnp.full((tm, LANES), -1.0, F32)
    one = jnp.ones((tm, LANES), F32)
    zero = jnp.zeros((tm, LANES), F32)
    q_fill_even = jnp.where((lane_t >= HEAD_DIM) & (lane_t < HEAD_DIM + N_SPLIT), neg_one, zero)
    q_fill_odd = jnp.where(lane_t < N_SPLIT, neg_one, zero)
    v_fill_even = jnp.where(lane_t == HEAD_DIM, one, zero)
    v_fill_odd = jnp.where(lane_t == 0, one, zero)
    for p in range(N_FOX_HEADS // 2):
        cols = slice(p * LANES, (p + 1) * LANES)
        zq = zf[:, cols] * (QK_SCALE * LOG2E)
        zk = zf[:, D_FOX + p * LANES:D_FOX + (p + 1) * LANES]
        zv = zf[:, 2 * D_FOX + p * LANES:2 * D_FOX + (p + 1) * LANES]
        ex = extras[:, cols]
        q_ref[0, 2 * p] = jnp.where(lo_half_t, zq, q_fill_even).astype(BF16)
        q_ref[0, 2 * p + 1] = jnp.where(lo_half_t, q_fill_odd, zq).astype(BF16)
        k_ref[0, 2 * p] = jnp.where(lo_half_t, zk, ex).astype(BF16)
        k_ref[0, 2 * p + 1] = jnp.where(lo_half_t, ex, zk).astype(BF16)
        v_ref[0, 2 * p] = jnp.where(lo_half_t, zv, v_fill_even).T.astype(BF16)
        v_ref[0, 2 * p + 1] = jnp.where(lo_half_t, v_fill_odd, zv).T.astype(BF16)


def _placement_matrix():
    m = np.zeros((N_SPLIT * LANES, D_FOX), np.float32)
    for h in range(N_FOX_HEADS):
        base = (h // 2) * LANES + (HEAD_DIM if h % 2 == 0 else 0)
        for i in range(N_SPLIT):
            m[i * LANES + h, base + i] = 1.0
    return m


def _group_mean_matrix():
    g = np.arange(D_GMLP) // HEAD_DIM
    return (g[:, None] == g[None, :]).astype(np.float32) / HEAD_DIM


def _mix_proj(x, layer, kv_all, p):
    b, s, d = x.shape
    tm = min(TM_PROJ, s)
    assert s % tm == 0 and tm % CHUNK == 0
    m = kv_all.shape[2]
    const2 = lambda shape: pl.BlockSpec(shape, lambda i, j: (0, 0))
    per_layer = lambda *shape: pl.BlockSpec((None,) + shape, lambda i, j: (layer,) + (0,) * len(shape))
    head_spec = pl.BlockSpec((1, N_FOX_HEADS, tm, LANES), lambda i, j: (i, 0, j, 0))
    head_shape = jax.ShapeDtypeStruct((b, N_FOX_HEADS, s, LANES), BF16)
    return pl.pallas_call(
        functools.partial(_mix_proj_kernel, tm=tm),
        grid=(b, s // tm),
        in_specs=[
            pl.BlockSpec((1, tm, d), lambda i, j: (i, j, 0)),
            per_layer(1, d),
            per_layer(d, W_ALL),
            per_layer(1, LANES),
            per_layer(1, D_GMLP),
            per_layer(N_GMLP_GROUPS, CHUNK, CHUNK),
            per_layer(CHUNK, D_GMLP),
            pl.BlockSpec((1, 1, m, 2 * D_MEMQ), lambda i, j: (layer, i, 0, 0)),
            const2((D_GMLP, D_GMLP)),
            const2((N_SPLIT * LANES, D_FOX)),
        ],
        out_specs=[
            pl.BlockSpec((1, tm, D_GMLP), lambda i, j: (i, j, 0)),
            pl.BlockSpec((1, tm, D_MEMQ), lambda i, j: (i, j, 0)),
            head_spec, head_spec,
            pl.BlockSpec((1, N_FOX_HEADS, LANES, tm), lambda i, j: (i, 0, 0, j)),
        ],
        out_shape=[
            jax.ShapeDtypeStruct((b, s, D_GMLP), BF16),
            jax.ShapeDtypeStruct((b, s, D_MEMQ), BF16),
            head_shape, head_shape,
            jax.ShapeDtypeStruct((b, N_FOX_HEADS, LANES, s), BF16),
        ],
        scratch_shapes=[pltpu.VMEM((1, LANES), F32)],
        compiler_params=pltpu.CompilerParams(
            dimension_semantics=("arbitrary", "arbitrary"), vmem_limit_bytes=VMEM_LIMIT),
        name="mix_proj",
    )(x, p["g_pre_mix"], p["w_all"], p["b_forget"], p["g_v"], p["w_s"], p["b_s"], kv_all, p["gmean"], p["place"])


def _fox_attn_kernel(q_ref, k_ref, vt_ref, o_ref, s0_ref, s1_ref, m_ref, acc_ref, *, tq):
    hps = q_ref.shape[1]
    n_q = q_ref.shape[2] // tq
    sub = lax.broadcasted_iota(jnp.int32, (LANES, tq), 0)
    bufs = (s0_ref, s1_ref)

    def scores(i, j, dst_ref):
        start = pl.multiple_of(j * tq, tq)
        for hh in range(hps):
            dst_ref[hh] = _dot_nt(k_ref[0, hh, pl.ds(start, tq), :], q_ref[0, hh, i * tq:(i + 1) * tq, :])

    def consume(j, src_ref):
        start = pl.multiple_of(j * tq, tq)
        for hh in range(hps):
            st = src_ref[hh]
            m = m_ref[hh]
            m_new = jnp.maximum(m, jnp.max(st, axis=0, keepdims=True))
            alpha = jnp.exp2(m - m_new)
            pt = jnp.exp2(st - m_new).astype(BF16)
            acc_ref[hh] = alpha * acc_ref[hh] + _dot(vt_ref[0, hh, :, pl.ds(start, tq)], pt)
            m_ref[hh] = m_new

    h = tq // 2
    tri = lax.broadcasted_iota(jnp.int32, (h, h), 0) <= lax.broadcasted_iota(jnp.int32, (h, h), 1)

    def scores_diag(i, dst_ref):
        base = i * tq
        for hh in range(hps):
            dst_ref[hh, :h, :] = _dot_nt(k_ref[0, hh, base:base + h, :], q_ref[0, hh, base:base + tq, :])
            dst_ref[hh, h:, h:] = _dot_nt(k_ref[0, hh, base + h:base + tq, :], q_ref[0, hh, base + h:base + tq, :])

    def consume_diag(i, src_ref):
        base = i * tq
        for hh in range(hps):
            s_ll = jnp.where(tri, src_ref[hh, :h, :h], NEG_INF)
            s_lh = src_ref[hh, :h, h:]
            s_hh = jnp.where(tri, src_ref[hh, h:, h:], NEG_INF)
            m_lo, m_hi = m_ref[hh, :, :h], m_ref[hh, :, h:]
            new_lo = jnp.maximum(m_lo, jnp.max(s_ll, axis=0, keepdims=True))
            new_hi = jnp.maximum(m_hi, jnp.maximum(jnp.max(s_lh, axis=0, keepdims=True),
                                                   jnp.max(s_hh, axis=0, keepdims=True)))
            p_ll = jnp.exp2(s_ll - new_lo).astype(BF16)
            p_hi = jnp.concatenate([jnp.exp2(s_lh - new_hi), jnp.exp2(s_hh - new_hi)], axis=0).astype(BF16)
            acc_ref[hh, :, :h] = (jnp.exp2(m_lo - new_lo) * acc_ref[hh, :, :h]
                                  + _dot(vt_ref[0, hh, :, base:base + h], p_ll))
            acc_ref[hh, :, h:] = (jnp.exp2(m_hi - new_hi) * acc_ref[hh, :, h:]
                                  + _dot(vt_ref[0, hh, :, base:base + tq], p_hi))
            m_ref[hh, :, :h] = new_lo
            m_ref[hh, :, h:] = new_hi

    def finish_tile(i):
        for p in range(hps // 2):
            outs = []
            for e in range(2):
                acc = acc_ref[2 * p + e]
                den_row = HEAD_DIM if e == 0 else 0
                outs.append(acc / acc[den_row:den_row + 1, :])
            pair_t = jnp.where(sub < HEAD_DIM, outs[0], outs[1])
            o_ref[0, i * tq:(i + 1) * tq, p * LANES:(p + 1) * LANES] = pair_t.T.astype(BF16)
        m_ref[...] = jnp.full(m_ref.shape, NEG_INF, F32)
        acc_ref[...] = jnp.zeros(acc_ref.shape, F32)

    m_ref[...] = jnp.full(m_ref.shape, NEG_INF, F32)
    acc_ref[...] = jnp.zeros(acc_ref.shape, F32)
    cur = 0
    scores_diag(0, bufs[0])
    for i in range(n_q):
        x_ref, y_ref = bufs[cur], bufs[1 - cur]

        def two_blocks(p, _, i=i, x_ref=x_ref, y_ref=y_ref):
            scores(i, 2 * p + 1, y_ref)
            consume(2 * p, x_ref)
            scores(i, 2 * p + 2, x_ref)
            consume(2 * p + 1, y_ref)
            return 0

        if i % 2 == 0:
            if i >= 2:
                if i >= 4:
                    lax.fori_loop(0, i // 2 - 1, two_blocks, 0)
                scores(i, i - 1, y_ref)
                consume(i - 2, x_ref)
                scores_diag(i, x_ref)
                consume(i - 1, y_ref)
            if i + 1 < n_q:
                scores(i + 1, 0, y_ref)
            consume_diag(i, x_ref)
            cur = 1 - cur
        else:
            if i >= 3:
                lax.fori_loop(0, i // 2, two_blocks, 0)
            scores_diag(i, y_ref)
            consume(i - 1, x_ref)
            if i + 1 < n_q:
                scores(i + 1, 0, x_ref)
            consume_diag(i, y_ref)
        finish_tile(i)


def _fox_attn(q, k, v):
    b, nh, s, _ = q.shape
    tq = min(TQ, s)
    hps = HEADS_PER_STEP
    assert s % tq == 0 and nh % hps == 0 and hps % 2 == 0
    qk_spec = pl.BlockSpec((1, hps, s, LANES), lambda i, p: (i, p, 0, 0))
    vt_spec = pl.BlockSpec((1, hps, LANES, s), lambda i, p: (i, p, 0, 0))
    return pl.pallas_call(
        functools.partial(_fox_attn_kernel, tq=tq),
        grid=(b, nh // hps),
        in_specs=[qk_spec, qk_spec, vt_spec],
        out_specs=pl.BlockSpec((1, s, hps * HEAD_DIM), lambda i, p: (i, 0, p)),
        out_shape=jax.ShapeDtypeStruct((b, s, nh * HEAD_DIM), BF16),
        scratch_shapes=[
            pltpu.VMEM((hps, tq, tq), F32), pltpu.VMEM((hps, tq, tq), F32),
            pltpu.VMEM((hps, 1, tq), F32), pltpu.VMEM((hps, LANES, tq), F32),
        ],
        compiler_params=pltpu.CompilerParams(
            dimension_semantics=("arbitrary", "arbitrary"), vmem_limit_bytes=VMEM_LIMIT),
        name="fox_attn",
    )(q, k, v)


def _out_ffn_kernel(x_ref, a_ref, b_ref, c_ref, woa_ref, wob_ref, woc_ref, g1_ref, g2_ref, g3_ref,
                    w1_ref, w2_ref, o_ref, *, ff_chunk):
    y = _dot(a_ref[...], woa_ref[...]) + _dot(b_ref[...], wob_ref[...]) + _dot(c_ref[...], woc_ref[...])
    x1 = x_ref[...] + _rmsnorm(y, g1_ref[...])
    h = _rmsnorm(x1, g2_ref[...]).astype(BF16)
    d_ff = w1_ref.shape[1]
    f = None
    for c in range(d_ff // ff_chunk):
        cols = slice(c * ff_chunk, (c + 1) * ff_chunk)
        a1 = jnp.maximum(_dot(h, w1_ref[:, cols]), 0.0)
        part = _dot((a1 * a1).astype(BF16), w2_ref[cols, :])
        f = part if f is None else f + part
    o_ref[...] = x1 + _rmsnorm(f, g3_ref[...])


def _out_ffn(x, a, b_, c, layer, p):
    t, d = x.shape
    tm = min(TM_FFN, t)
    assert t % tm == 0
    d_ff = p["w_ff1"].shape[2]
    ff_chunk = min(FF_CHUNK, d_ff)
    assert D_GMLP == D_FOX and (D_GMLP + D_FOX) % D_MEMQ == 0
    row = lambda w: pl.BlockSpec((tm, w), lambda i: (i, 0))
    per_layer = lambda shape, blk=0: pl.BlockSpec(
        (None,) + shape, lambda i: (layer, blk) + (0,) * (len(shape) - 1), pipeline_mode=pl.Buffered(1))
    return pl.pallas_call(
        functools.partial(_out_ffn_kernel, ff_chunk=ff_chunk),
        grid=(t // tm,),
        in_specs=[
            row(d), row(D_GMLP), row(D_FOX), row(D_MEMQ),
            per_layer((D_GMLP, d), 0), per_layer((D_FOX, d), 1), per_layer((D_MEMQ, d), (D_GMLP + D_FOX) // D_MEMQ),
            per_layer((1, d)), per_layer((1, d)), per_layer((1, d)),
            per_layer((d, d_ff)), per_layer((d_ff, d)),
        ],
        out_specs=row(d),
        out_shape=jax.ShapeDtypeStruct((t, d), F32),
        compiler_params=pltpu.CompilerParams(
            dimension_semantics=("arbitrary",), vmem_limit_bytes=VMEM_LIMIT),
        name="out_ffn",
    )(x, a, b_, c, p["w_out"], p["w_out"], p["w_out"], p["g_post_mix"], p["g_pre_ffn"], p["g_post_ffn"],
      p["w_ff1"], p["w_ff2"])


def _pack_params(norm_pre_mix, norm_post_mix, norm_pre_ffn, norm_post_ffn, w_in, b_forget,
                 gmlp_v_norm, gmlp_w_s, gmlp_b_s, w_out, w_ff1, w_ff2):
    depth, d, _ = w_in.shape
    i_mq = O_GATE + N_FOX_HEADS
    pad = LANES - N_FOX_HEADS
    w_all = jnp.concatenate(
        [w_in[:, :, :i_mq].astype(BF16), jnp.zeros((depth, d, pad), BF16), w_in[:, :, i_mq:].astype(BF16)], axis=2)
    return {
        "g_pre_mix": norm_pre_mix.reshape(depth, 1, d),
        "w_all": w_all,
        "b_forget": jnp.pad(b_forget, ((0, 0), (0, pad))).reshape(depth, 1, LANES),
        "g_v": gmlp_v_norm.reshape(depth, 1, D_GMLP),
        "w_s": gmlp_w_s,
        "b_s": jnp.repeat(jnp.swapaxes(gmlp_b_s, 1, 2), HEAD_DIM, axis=2),
        "gmean": jnp.asarray(_group_mean_matrix(), BF16),
        "place": jnp.asarray(_placement_matrix(), BF16),
        "w_out": w_out.astype(BF16),
        "g_post_mix": norm_post_mix.reshape(depth, 1, d),
        "g_pre_ffn": norm_pre_ffn.reshape(depth, 1, d),
        "g_post_ffn": norm_post_ffn.reshape(depth, 1, d),
        "w_ff1": w_ff1.astype(BF16),
        "w_ff2": w_ff2.astype(BF16),
    }


def kernel(x, mem, norm_pre_mix, norm_post_mix, norm_pre_ffn, norm_post_ffn, norm_mem, w_in, b_forget,
           gmlp_v_norm, gmlp_w_s, gmlp_b_s, w_mem_kv, w_out, w_ff1, w_ff2):
    b, s, d = x.shape
    depth = w_in.shape[0]
    p = _pack_params(norm_pre_mix, norm_post_mix, norm_pre_ffn, norm_post_ffn, w_in, b_forget,
                     gmlp_v_norm, gmlp_w_s, gmlp_b_s, w_out, w_ff1, w_ff2)
    kv_all = _memkv(mem, norm_mem, w_mem_kv.astype(BF16))
    for l in range(depth):
        out_a, out_c, q, k, v = _mix_proj(x, l, kv_all, p)
        out_b = _fox_attn(q, k, v)
        x = _out_ffn(x.reshape(b * s, d), out_a.reshape(b * s, D_GMLP), out_b.reshape(b * s, D_FOX),
                     out_c.reshape(b * s, D_MEMQ), l, p).reshape(b, s, d)
    return x
```

```python
import functools
import math

import numpy as np
import jax
import jax.numpy as jnp
from jax import lax
from jax.experimental import pallas as pl
from jax.experimental.pallas import tpu as pltpu

F32 = jnp.float32
BF16 = jnp.bfloat16

HEAD_DIM = 64
LANES = 128
N_GMLP_GROUPS = 6
N_FOX_HEADS = 6
N_MEM_HEADS = 4
D_GMLP = N_GMLP_GROUPS * HEAD_DIM
D_FOX = N_FOX_HEADS * HEAD_DIM
D_MEMQ = N_MEM_HEADS * HEAD_DIM
CHUNK = 128
RMS_EPS = 1e-6
NEG_INF = -1e30
QK_SCALE = HEAD_DIM ** -0.5
LOG2E = math.log2(math.e)
N_SPLIT = 3
O_FX = 2 * D_GMLP
O_GATE = O_FX + 3 * D_FOX
O_MQ = O_GATE + LANES
W_ALL = O_MQ + D_MEMQ
MXU_TILE = 256
O_SPLIT = O_GATE - (O_GATE - O_FX) % MXU_TILE
assert O_FX % MXU_TILE == 0 and (O_SPLIT - O_FX) % MXU_TILE == 0 and (W_ALL - O_SPLIT) % MXU_TILE == 0

V7X_VMEM_BYTES = 64 * 1024 * 1024
VMEM_LIMIT = V7X_VMEM_BYTES - 8 * 1024 * 1024

TM_PROJ = 1024
TQ = 512
HEADS_PER_STEP = 2
TM_FFN = 1024
FFN_SUB_ROWS = 256
FF_CHUNK = 1024


def _dot(a, b):
    return jnp.dot(a, b, preferred_element_type=F32)


def _dot_nt(a, b):
    return lax.dot_general(a, b, (((1,), (1,)), ((), ())), preferred_element_type=F32)


def _rmsnorm(x, g):
    return x * lax.rsqrt(jnp.mean(x * x, axis=-1, keepdims=True) + RMS_EPS) * g


def _gelu_tanh(x):
    c = math.sqrt(2.0 / math.pi)
    return 0.5 * x * (1.0 + jnp.tanh(c * (x + 0.044715 * (x * x * x))))


def _log_sigmoid(x):
    return -(jnp.maximum(-x, 0.0) + jnp.log1p(jnp.exp(-jnp.abs(x))))


def _split_bf16(x):
    pieces = []
    r = x
    for _ in range(N_SPLIT):
        p = r.astype(BF16)
        pieces.append(p)
        r = r - p.astype(F32)
    return pieces


def _memkv_kernel(mem_ref, g_ref, w_ref, o_ref):
    mn = _rmsnorm(mem_ref[...], g_ref[...]).astype(BF16)
    o_ref[...] = _dot(mn, w_ref[...]).astype(BF16)


def _memkv(mem, norm_mem, w_kv_bf16):
    depth = w_kv_bf16.shape[0]
    b, m, d = mem.shape
    n = w_kv_bf16.shape[-1]
    kv = pl.pallas_call(
        _memkv_kernel,
        grid=(depth,),
        in_specs=[
            pl.BlockSpec((b * m, d), lambda l: (0, 0)),
            pl.BlockSpec((None, 1, d), lambda l: (l, 0, 0)),
            pl.BlockSpec((None, d, n), lambda l: (l, 0, 0)),
        ],
        out_specs=pl.BlockSpec((None, b * m, n), lambda l: (l, 0, 0)),
        out_shape=jax.ShapeDtypeStruct((depth, b * m, n), BF16),
        compiler_params=pltpu.CompilerParams(dimension_semantics=("arbitrary",), vmem_limit_bytes=VMEM_LIMIT),
        name="mem_kv",
    )(mem.reshape(b * m, d), norm_mem.reshape(depth, 1, d), w_kv_bf16)
    return kv.reshape(depth, b, m, n)


def _mix_proj_kernel(x_ref, gpre_ref, wall_ref, bf_ref, gv_ref, ws_ref, bs_ref, kv_ref, gmean_ref, place_ref,
                     oa_ref, oc_ref, q_ref, k_ref, v_ref, carry_ref, *, tm):
    n_chunks = tm // CHUNK
    assert n_chunks % 2 == 0

    @pl.when(pl.program_id(1) == 0)
    def _():
        carry_ref[...] = jnp.zeros_like(carry_ref)

    lane =lax.broadcasted_iota(jnp.int32, (CHUNK, LANES), 1)
    lane_t = lax.broadcasted_iota(jnp.int32, (tm, LANES), 1)
    lo_half_t = lane_t < HEAD_DIM
    row = lax.broadcasted_iota(jnp.int32, (CHUNK, CHUNK), 0)
    col = lax.broadcasted_iota(jnp.int32, (CHUNK, CHUNK), 1)
    causal = row >= col

    h = _rmsnorm(x_ref[0], gpre_ref[...]).astype(BF16)
    zg = _dot(h, wall_ref[:, :O_FX])
    z_tail = _dot(h, wall_ref[:, O_SPLIT:])
    z_mid = _dot(h, wall_ref[:, O_FX:O_SPLIT])
    zgate = z_tail[:, O_GATE - O_SPLIT:O_MQ - O_SPLIT]
    zm = z_tail[:, O_MQ - O_SPLIT:] * QK_SCALE

    def fox_cols(start):
        col = O_FX + start
        return z_mid[:, col - O_FX:col - O_FX + LANES] if col < O_SPLIT else z_tail[:, col - O_SPLIT:col - O_SPLIT + LANES]

    u = _gelu_tanh(zg[:, :D_GMLP])
    v = _gelu_tanh(zg[:, D_GMLP:])
    ms = _dot((v * v).astype(BF16), gmean_ref[...])

    kv = kv_ref[0, 0]
    mem_scores = []
    for hd in range(N_MEM_HEADS):
        p, e = divmod(hd, 2)
        keep = lo_half_t if e == 0 else jnp.logical_not(lo_half_t)
        qh = jnp.where(keep, zm[:, p * LANES:(p + 1) * LANES], 0.0).astype(BF16)
        mem_scores.append(_dot_nt(qh, kv[:, p * LANES:(p + 1) * LANES]))

    log_f = _log_sigmoid(zgate + bf_ref[...])
    tril = jnp.where(causal, 1.0, 0.0).astype(BF16)
    carry = carry_ref[...]
    cums = []
    for c in range(n_chunks):
        pieces = _split_bf16(log_f[c * CHUNK:(c + 1) * CHUNK])
        cum3 = _dot(tril, jnp.concatenate(pieces, axis=1))
        cum = carry
        for i in range(N_SPLIT):
            cum = cum + cum3[:, i * LANES:(i + 1) * LANES]
        carry = cum[CHUNK - 1:CHUNK, :]
        cums.append(cum)
    carry_ref[...] = carry
    c_all = jnp.concatenate(cums, axis=0)

    vn = (v * lax.rsqrt(ms + RMS_EPS) * gv_ref[...]).astype(BF16)
    w_pairs = []
    for p in range(N_GMLP_GROUPS // 2):
        w0 = jnp.where(causal, ws_ref[2 * p], 0.0).astype(BF16)
        w1 = jnp.where(causal, ws_ref[2 * p + 1], 0.0).astype(BF16)
        w_pairs.append(jnp.concatenate([w0, w1], axis=0))
    for cp in range(n_chunks // 2):
        rows0 = slice(2 * cp * CHUNK, (2 * cp + 1) * CHUNK)
        rows1 = slice((2 * cp + 1) * CHUNK, (2 * cp + 2) * CHUNK)
        parts0, parts1 = [], []
        for p in range(N_GMLP_GROUPS // 2):
            cols = slice(p * LANES, (p + 1) * LANES)
            r = _dot(w_pairs[p], jnp.concatenate([vn[rows0, cols], vn[rows1, cols]], axis=1))
            parts0.append(jnp.where(lane < HEAD_DIM, r[:CHUNK, :LANES], r[CHUNK:, :LANES]))
            parts1.append(jnp.where(lane < HEAD_DIM, r[:CHUNK, LANES:], r[CHUNK:, LANES:]))
        oa_ref[0, rows0, :] = (u[rows0] * (jnp.concatenate(parts0, axis=1) + bs_ref[...])).astype(BF16)
        oa_ref[0, rows1, :] = (u[rows1] * (jnp.concatenate(parts1, axis=1) + bs_ref[...])).astype(BF16)

    for p in range(N_MEM_HEADS // 2):
        vmp = kv[:, D_MEMQ + p * LANES:D_MEMQ + (p + 1) * LANES]
        outs = []
        for e in range(2):
            s = mem_scores[2 * p + e]
            pe = jnp.exp(s - jnp.max(s, axis=-1, keepdims=True))
            den = jnp.sum(pe, axis=-1, keepdims=True)
            outs.append(_dot(pe.astype(BF16), vmp) / den)
        oc_ref[0, :, p * LANES:(p + 1) * LANES] = jnp.where(lo_half_t, outs[0], outs[1]).astype(BF16)

    extras = _dot(jnp.concatenate(_split_bf16(c_all * LOG2E), axis=1), place_ref[...])
    neg_one = jnp.full((tm, LANES), -1.0, F32)
    one = jnp.ones((tm, LANES), F32)
    zero = jnp.zeros((tm, LANES), F32)
    q_fill_even = jnp.where((lane_t >= HEAD_DIM) & (lane_t < HEAD_DIM + N_SPLIT), neg_one, zero)
    q_fill_odd = jnp.where(lane_t < N_SPLIT, neg_one, zero)
    v_fill_even = jnp.where(lane_t == HEAD_DIM, one, zero)
    v_fill_odd = jnp.where(lane_t == 0, one, zero)
    for p in range(N_FOX_HEADS // 2):
        cols = slice(p * LANES, (p + 1) * LANES)
        zq = fox_cols(p * LANES) * (QK_SCALE * LOG2E)
        zk = fox_cols(D_FOX + p * LANES)
        zv = fox_cols(2 * D_FOX + p * LANES)
        ex = extras[:, cols]
        q_ref[0, 2 * p] = jnp.where(lo_half_t, zq, q_fill_even).astype(BF16)
        q_ref[0, 2 * p + 1] = jnp.where(lo_half_t, q_fill_odd, zq).astype(BF16)
        k_ref[0, 2 * p] = jnp.where(lo_half_t, zk, ex).astype(BF16)
        k_ref[0, 2 * p + 1] = jnp.where(lo_half_t, ex, zk).astype(BF16)
        v_ref[0, 2 * p] = jnp.where(lo_half_t, zv, v_fill_even).T.astype(BF16)
        v_ref[0, 2 * p + 1] = jnp.where(lo_half_t, v_fill_odd, zv).T.astype(BF16)


def _placement_matrix():
    m = np.zeros((N_SPLIT * LANES, D_FOX), np.float32)
    for h in range(N_FOX_HEADS):
        base = (h // 2) * LANES + (HEAD_DIM if h % 2 == 0 else 0)
        for i in range(N_SPLIT):
            m[i * LANES + h, base + i] = 1.0
    return m


def _group_mean_matrix():
    g = np.arange(D_GMLP) // HEAD_DIM
    return (g[:, None] == g[None, :]).astype(np.float32) / HEAD_DIM


def _mix_proj(x, layer, kv_all, p):
    b, s, d = x.shape
    tm = min(TM_PROJ, s)
    assert s % tm == 0 and tm % CHUNK == 0
    m = kv_all.shape[2]
    const2 = lambda shape: pl.BlockSpec(shape, lambda i, j: (0, 0))
    per_layer = lambda *shape: pl.BlockSpec((None,) + shape, lambda i, j: (layer,) + (0,) * len(shape))
    head_spec = pl.BlockSpec((1, N_FOX_HEADS, tm, LANES), lambda i, j: (i, 0, j, 0))
    head_shape = jax.ShapeDtypeStruct((b, N_FOX_HEADS, s, LANES), BF16)
    return pl.pallas_call(
        functools.partial(_mix_proj_kernel, tm=tm),
        grid=(b, s // tm),
        in_specs=[
            pl.BlockSpec((1, tm, d), lambda i, j: (i, j, 0)),
            per_layer(1, d),
            per_layer(d, W_ALL),
            per_layer(1, LANES),
            per_layer(1, D_GMLP),
            per_layer(N_GMLP_GROUPS, CHUNK, CHUNK),
            per_layer(CHUNK, D_GMLP),
            pl.BlockSpec((1, 1, m, 2 * D_MEMQ), lambda i, j: (layer, i, 0, 0)),
            const2((D_GMLP, D_GMLP)),
            const2((N_SPLIT * LANES, D_FOX)),
        ],
        out_specs=[
            pl.BlockSpec((1, tm, D_GMLP), lambda i, j: (i, j, 0)),
            pl.BlockSpec((1, tm, D_MEMQ), lambda i, j: (i, j, 0)),
            head_spec, head_spec,
            pl.BlockSpec((1, N_FOX_HEADS, LANES, tm), lambda i, j: (i, 0, 0, j)),
        ],
        out_shape=[
            jax.ShapeDtypeStruct((b, s, D_GMLP), BF16),
            jax.ShapeDtypeStruct((b, s, D_MEMQ), BF16),
            head_shape, head_shape,
            jax.ShapeDtypeStruct((b, N_FOX_HEADS, LANES, s), BF16),
        ],
        scratch_shapes=[pltpu.VMEM((1, LANES), F32)],
        compiler_params=pltpu.CompilerParams(
            dimension_semantics=("arbitrary", "arbitrary"), vmem_limit_bytes=VMEM_LIMIT),
        name="mix_proj",
    )(x, p["g_pre_mix"], p["w_all"], p["b_forget"], p["g_v"], p["w_s"], p["b_s"], kv_all, p["gmean"], p["place"])


def _fox_attn_kernel(q_ref, k_ref, vt_ref, o_ref, s0_ref, s1_ref, m_ref, acc_ref, *, tq):
    hps = q_ref.shape[1]
    n_q = q_ref.shape[2] // tq
    sub = lax.broadcasted_iota(jnp.int32, (LANES, tq), 0)
    bufs = (s0_ref, s1_ref)

    def scores(i, j, dst_ref):
        start = pl.multiple_of(j * tq, tq)
        for hh in range(hps):
            dst_ref[hh] = _dot_nt(k_ref[0, hh, pl.ds(start, tq), :], q_ref[0, hh, i * tq:(i + 1) * tq, :])

    def consume(j, src_ref):
        start = pl.multiple_of(j * tq, tq)
        for hh in range(hps):
            st = src_ref[hh]
            m = m_ref[hh]
            m_new = jnp.maximum(m, jnp.max(st, axis=0, keepdims=True))
            alpha = jnp.exp2(m - m_new)
            pt = jnp.exp2(st - m_new).astype(BF16)
            acc_ref[hh] = alpha * acc_ref[hh] + _dot(vt_ref[0, hh, :, pl.ds(start, tq)], pt)
            m_ref[hh] = m_new

    h = tq // 2
    tri = lax.broadcasted_iota(jnp.int32, (h, h), 0) <= lax.broadcasted_iota(jnp.int32, (h, h), 1)

    def scores_diag(i, dst_ref):
        base = i * tq
        for hh in range(hps):
            dst_ref[hh, :h, :] = _dot_nt(k_ref[0, hh, base:base + h, :], q_ref[0, hh, base:base + tq, :])
            dst_ref[hh, h:, h:] = _dot_nt(k_ref[0, hh, base + h:base + tq, :], q_ref[0, hh, base + h:base + tq, :])

    def consume_diag(i, src_ref):
        base = i * tq
        for hh in range(hps):
            s_ll = jnp.where(tri, src_ref[hh, :h, :h], NEG_INF)
            s_lh = src_ref[hh, :h, h:]
            s_hh = jnp.where(tri, src_ref[hh, h:, h:], NEG_INF)
            m_lo, m_hi = m_ref[hh, :, :h], m_ref[hh, :, h:]
            new_lo = jnp.maximum(m_lo, jnp.max(s_ll, axis=0, keepdims=True))
            new_hi = jnp.maximum(m_hi, jnp.maximum(jnp.max(s_lh, axis=0, keepdims=True),
                                                   jnp.max(s_hh, axis=0, keepdims=True)))
            p_ll = jnp.exp2(s_ll - new_lo).astype(BF16)
            p_hi = jnp.concatenate([jnp.exp2(s_lh - new_hi), jnp.exp2(s_hh - new_hi)], axis=0).astype(BF16)
            acc_ref[hh, :, :h] = (jnp.exp2(m_lo - new_lo) * acc_ref[hh, :, :h]
                                  + _dot(vt_ref[0, hh, :, base:base + h], p_ll))
            acc_ref[hh, :, h:] = (jnp.exp2(m_hi - new_hi) * acc_ref[hh, :, h:]
                                  + _dot(vt_ref[0, hh, :, base:base + tq], p_hi))
            m_ref[hh, :, :h] = new_lo
            m_ref[hh, :, h:] = new_hi

    def finish_tile(i):
        for p in range(hps // 2):
            outs = []
            for e in range(2):
                acc = acc_ref[2 * p + e]
                den_row = HEAD_DIM if e == 0 else 0
                outs.append(acc / acc[den_row:den_row + 1, :])
            pair_t = jnp.where(sub < HEAD_DIM, outs[0], outs[1])
            o_ref[0, i * tq:(i + 1) * tq, p * LANES:(p + 1) * LANES] = pair_t.T.astype(BF16)
        m_ref[...] = jnp.full(m_ref.shape, NEG_INF, F32)
        acc_ref[...] = jnp.zeros(acc_ref.shape, F32)

    m_ref[...] = jnp.full(m_ref.shape, NEG_INF, F32)
    acc_ref[...] = jnp.zeros(acc_ref.shape, F32)
    cur = 0
    scores_diag(0, bufs[0])
    for i in range(n_q):
        x_ref, y_ref = bufs[cur], bufs[1 - cur]

        def two_blocks(p, _, i=i, x_ref=x_ref, y_ref=y_ref):
            scores(i, 2 * p + 1, y_ref)
            consume(2 * p, x_ref)
            scores(i, 2 * p + 2, x_ref)
            consume(2 * p + 1, y_ref)
            return 0

        if i % 2 == 0:
            if i >= 2:
                if i >= 4:
                    lax.fori_loop(0, i // 2 - 1, two_blocks, 0)
                scores(i, i - 1, y_ref)
                consume(i - 2, x_ref)
                scores_diag(i, x_ref)
                consume(i - 1, y_ref)
            if i + 1 < n_q:
                scores(i + 1, 0, y_ref)
            consume_diag(i, x_ref)
            cur = 1 - cur
        else:
            if i >= 3:
                lax.fori_loop(0, i // 2, two_blocks, 0)
            scores_diag(i, y_ref)
            consume(i - 1, x_ref)
            if i + 1 < n_q:
                scores(i + 1, 0, x_ref)
            consume_diag(i, y_ref)
        finish_tile(i)


def _fox_attn(q, k, v):
    b, nh, s, _ = q.shape
    tq = min(TQ, s)
    hps = HEADS_PER_STEP
    assert s % tq == 0 and nh % hps == 0 and hps % 2 == 0
    qk_spec = pl.BlockSpec((1, hps, s, LANES), lambda i, p: (i, p, 0, 0))
    vt_spec = pl.BlockSpec((1, hps, LANES, s), lambda i, p: (i, p, 0, 0))
    return pl.pallas_call(
        functools.partial(_fox_attn_kernel, tq=tq),
        grid=(b, nh // hps),
        in_specs=[qk_spec, qk_spec, vt_spec],
        out_specs=pl.BlockSpec((1, s, hps * HEAD_DIM), lambda i, p: (i, 0, p)),
        out_shape=jax.ShapeDtypeStruct((b, s, nh * HEAD_DIM), BF16),
        scratch_shapes=[
            pltpu.VMEM((hps, tq, tq), F32), pltpu.VMEM((hps, tq, tq), F32),
            pltpu.VMEM((hps, 1, tq), F32), pltpu.VMEM((hps, LANES, tq), F32),
        ],
        compiler_params=pltpu.CompilerParams(
            dimension_semantics=("arbitrary", "arbitrary"), vmem_limit_bytes=VMEM_LIMIT),
        name="fox_attn",
    )(q, k, v)


def _out_ffn_kernel(x_ref, a_ref, b_ref, c_ref, woa_ref, wob_ref, woc_ref, g1_ref, g2_ref, g3_ref,
                    w1_ref, w2_ref, o_ref, *, ff_chunk, n_sub):
    tm = x_ref.shape[0]
    sub = tm // n_sub
    d_ff = w1_ref.shape[1]
    ys = []
    for s in range(n_sub):
        rows = slice(s * sub, (s + 1) * sub)
        ys.append(_dot(a_ref[rows, :], woa_ref[...]) + _dot(b_ref[rows, :], wob_ref[...])
                  + _dot(c_ref[rows, :], woc_ref[...]))
    for s in range(n_sub):
        rows = slice(s * sub, (s + 1) * sub)
        x1 = x_ref[rows, :] + _rmsnorm(ys[s], g1_ref[...])
        h = _rmsnorm(x1, g2_ref[...]).astype(BF16)
        f = None
        for c in range(d_ff // ff_chunk):
            cols = slice(c * ff_chunk, (c + 1) * ff_chunk)
            a1 = jnp.maximum(_dot(h, w1_ref[:, cols]), 0.0)
            part = _dot((a1 * a1).astype(BF16), w2_ref[cols, :])
            f = part if f is None else f + part
        o_ref[rows, :] = x1 + _rmsnorm(f, g3_ref[...])


def _out_ffn(x, a, b_, c, layer, p):
    t, d = x.shape
    tm = min(TM_FFN, t)
    assert t % tm == 0
    d_ff = p["w_ff1"].shape[2]
    ff_chunk = min(FF_CHUNK, d_ff)
    assert D_GMLP == D_FOX and (D_GMLP + D_FOX) % D_MEMQ == 0
    row = lambda w: pl.BlockSpec((tm, w), lambda i: (i, 0))
    per_layer = lambda shape, blk=0: pl.BlockSpec(
        (None,) + shape, lambda i: (layer, blk) + (0,) * (len(shape) - 1), pipeline_mode=pl.Buffered(1))
    return pl.pallas_call(
        functools.partial(_out_ffn_kernel, ff_chunk=ff_chunk, n_sub=max(1, tm // FFN_SUB_ROWS)),
        grid=(t // tm,),
        in_specs=[
            row(d), row(D_GMLP), row(D_FOX), row(D_MEMQ),
            per_layer((D_GMLP, d), 0), per_layer((D_FOX, d), 1), per_layer((D_MEMQ, d), (D_GMLP + D_FOX) // D_MEMQ),
            per_layer((1, d)), per_layer((1, d)), per_layer((1, d)),
            per_layer((d, d_ff)), per_layer((d_ff, d)),
        ],
        out_specs=row(d),
        out_shape=jax.ShapeDtypeStruct((t, d), F32),
        compiler_params=pltpu.CompilerParams(
            dimension_semantics=("arbitrary",), vmem_limit_bytes=VMEM_LIMIT),
        name="out_ffn",
    )(x, a, b_, c, p["w_out"], p["w_out"], p["w_out"], p["g_post_mix"], p["g_pre_ffn"], p["g_post_ffn"],
      p["w_ff1"], p["w_ff2"])


def _pack_params(norm_pre_mix, norm_post_mix, norm_pre_ffn, norm_post_ffn, w_in, b_forget,
                 gmlp_v_norm, gmlp_w_s, gmlp_b_s, w_out, w_ff1, w_ff2):
    depth, d, _ = w_in.shape
    i_mq = O_GATE + N_FOX_HEADS
    pad = LANES - N_FOX_HEADS
    w_all = jnp.concatenate(
        [w_in[:, :, :i_mq].astype(BF16), jnp.zeros((depth, d, pad), BF16), w_in[:, :, i_mq:].astype(BF16)], axis=2)
    return {
        "g_pre_mix": norm_pre_mix.reshape(depth, 1, d),
        "w_all": w_all,
        "b_forget": jnp.pad(b_forget, ((0, 0), (0, pad))).reshape(depth, 1, LANES),
        "g_v": gmlp_v_norm.reshape(depth, 1, D_GMLP),
        "w_s": gmlp_w_s,
        "b_s": jnp.repeat(jnp.swapaxes(gmlp_b_s, 1, 2), HEAD_DIM, axis=2),
        "gmean": jnp.asarray(_group_mean_matrix(), BF16),
        "place": jnp.asarray(_placement_matrix(), BF16),
        "w_out": w_out.astype(BF16),
        "g_post_mix": norm_post_mix.reshape(depth, 1, d),
        "g_pre_ffn": norm_pre_ffn.reshape(depth, 1, d),
        "g_post_ffn": norm_post_ffn.reshape(depth, 1, d),
        "w_ff1": w_ff1.astype(BF16),
        "w_ff2": w_ff2.astype(BF16),
    }


def kernel(x, mem, norm_pre_mix, norm_post_mix, norm_pre_ffn, norm_post_ffn, norm_mem, w_in, b_forget,
           gmlp_v_norm, gmlp_w_s, gmlp_b_s, w_mem_kv, w_out, w_ff1, w_ff2):
    b, s, d = x.shape
    depth = w_in.shape[0]
    p = _pack_params(norm_pre_mix, norm_post_mix, norm_pre_ffn, norm_post_ffn, w_in, b_forget,
                     gmlp_v_norm, gmlp_w_s, gmlp_b_s, w_out, w_ff1, w_ff2)
    kv_all = _memkv(mem, norm_mem, w_mem_kv.astype(BF16))
    for l in range(depth):
        out_a, out_c, q, k, v = _mix_proj(x, l, kv_all, p)
        out_b = _fox_attn(q, k, v)
        x = _out_ffn(x.reshape(b * s, d), out_a.reshape(b * s, D_GMLP), out_b.reshape(b * s, D_FOX),
                     out_c.reshape(b * s, D_MEMQ), l, p).reshape(b, s, d)
    return x
```

```python
import functools
import math

import numpy as np
import jax
import jax.numpy as jnp
from jax import lax
from jax.experimental import pallas as pl
from jax.experimental.pallas import tpu as pltpu

F32 = jnp.float32
BF16 = jnp.bfloat16

HEAD_DIM = 64
LANES = 128
N_GMLP_GROUPS = 6
N_FOX_HEADS = 6
N_MEM_HEADS = 4
D_GMLP = N_GMLP_GROUPS * HEAD_DIM
D_FOX = N_FOX_HEADS * HEAD_DIM
D_MEMQ = N_MEM_HEADS * HEAD_DIM
CHUNK = 128
RMS_EPS = 1e-6
NEG_INF = -1e30
QK_SCALE = HEAD_DIM ** -0.5
LOG2E = math.log2(math.e)
N_SPLIT = 3
PIECE_STRIDE = 8
O_FX = 2 * D_GMLP
O_GATE = O_FX + 3 * D_FOX
O_MQ = O_GATE + LANES
W_ALL = O_MQ + D_MEMQ
MXU_TILE = 256
O_SPLIT = O_GATE - (O_GATE - O_FX) % MXU_TILE
assert O_FX % MXU_TILE == 0 and (O_SPLIT - O_FX) % MXU_TILE == 0 and (W_ALL - O_SPLIT) % MXU_TILE == 0

V7X_VMEM_BYTES = 64 * 1024 * 1024
VMEM_LIMIT = V7X_VMEM_BYTES - 8 * 1024 * 1024

TM_PROJ = 1024
TQ = 512
HEADS_PER_STEP = 2
TM_FFN = 1024
FFN_SUB_ROWS = 256
FF_CHUNK = 1024


def _dot(a, b):
    return jnp.dot(a, b, preferred_element_type=F32)


def _dot_nt(a, b):
    return lax.dot_general(a, b, (((1,), (1,)), ((), ())), preferred_element_type=F32)


def _rmsnorm(x, g):
    return x * lax.rsqrt(jnp.mean(x * x, axis=-1, keepdims=True) + RMS_EPS) * g


def _gelu_tanh(x):
    c = math.sqrt(2.0 / math.pi)
    return 0.5 * x * (1.0 + jnp.tanh(c * (x + 0.044715 * (x * x * x))))


def _log_sigmoid(x):
    return -(jnp.maximum(-x, 0.0) + jnp.log1p(jnp.exp(-jnp.abs(x))))


def _split_bf16(x):
    pieces = []
    r = x
    for _ in range(N_SPLIT):
        p = r.astype(BF16)
        pieces.append(p)
        r = r - p.astype(F32)
    return pieces


def _memkv_kernel(mem_ref, g_ref, w_ref, o_ref):
    mn = _rmsnorm(mem_ref[...], g_ref[...]).astype(BF16)
    o_ref[...] = _dot(mn, w_ref[...]).astype(BF16)


def _memkv(mem, norm_mem, w_kv_bf16):
    depth = w_kv_bf16.shape[0]
    b, m, d = mem.shape
    n = w_kv_bf16.shape[-1]
    kv = pl.pallas_call(
        _memkv_kernel,
        grid=(depth,),
        in_specs=[
            pl.BlockSpec((b * m, d), lambda l: (0, 0)),
            pl.BlockSpec((None, 1, d), lambda l: (l, 0, 0)),
            pl.BlockSpec((None, d, n), lambda l: (l, 0, 0)),
        ],
        out_specs=pl.BlockSpec((None, b * m, n), lambda l: (l, 0, 0)),
        out_shape=jax.ShapeDtypeStruct((depth, b * m, n), BF16),
        compiler_params=pltpu.CompilerParams(dimension_semantics=("arbitrary",), vmem_limit_bytes=VMEM_LIMIT),
        name="mem_kv",
    )(mem.reshape(b * m, d), norm_mem.reshape(depth, 1, d), w_kv_bf16)
    return kv.reshape(depth, b, m, n)


def _mix_proj_kernel(x_ref, gpre_ref, wall_ref, bf_ref, gv_ref, ws_ref, bs_ref, kv_ref, gmean_ref, place_ref,
                     oa_ref, oc_ref, q_ref, k_ref, v_ref, carry_ref, *, tm):
    n_chunks = tm // CHUNK
    assert n_chunks % 2 == 0

    @pl.when(pl.program_id(1) == 0)
    def _():
        carry_ref[...] = jnp.zeros_like(carry_ref)

    lane =lax.broadcasted_iota(jnp.int32, (CHUNK, LANES), 1)
    lane_t = lax.broadcasted_iota(jnp.int32, (tm, LANES), 1)
    lo_half_t = lane_t < HEAD_DIM
    row = lax.broadcasted_iota(jnp.int32, (CHUNK, CHUNK), 0)
    col = lax.broadcasted_iota(jnp.int32, (CHUNK, CHUNK), 1)
    causal = row >= col

    h = _rmsnorm(x_ref[0], gpre_ref[...]).astype(BF16)
    zg = _dot(h, wall_ref[:, :O_FX])
    z_tail = _dot(h, wall_ref[:, O_SPLIT:])
    o_half = O_FX + (O_SPLIT - O_FX) // 2
    assert (o_half - O_FX) % MXU_TILE == 0
    z_parts = {O_FX: _dot(h, wall_ref[:, O_FX:o_half]), O_SPLIT: z_tail}
    zgate = z_tail[:, O_GATE - O_SPLIT:O_MQ - O_SPLIT]
    zm = z_tail[:, O_MQ - O_SPLIT:] * QK_SCALE

    def fox_cols(start):
        col = O_FX + start
        base = max(b for b in z_parts if b <= col)
        return z_parts[base][:, col - base:col - base + LANES]

    u = _gelu_tanh(zg[:, :D_GMLP])
    v = _gelu_tanh(zg[:, D_GMLP:])
    sq = (v * v).astype(BF16)
    split = (D_GMLP // MXU_TILE) * MXU_TILE
    ms = jnp.concatenate([_dot(sq[:, :split], gmean_ref[:split, :split]),
                          _dot(sq[:, split:], gmean_ref[split:, split:])], axis=1)

    kv = kv_ref[0, 0]
    mem_scores = []
    for hd in range(N_MEM_HEADS):
        p, e = divmod(hd, 2)
        keep = lo_half_t if e == 0 else jnp.logical_not(lo_half_t)
        qh = jnp.where(keep, zm[:, p * LANES:(p + 1) * LANES], 0.0).astype(BF16)
        mem_scores.append(_dot_nt(qh, kv[:, p * LANES:(p + 1) * LANES]))

    log_f = _log_sigmoid(zgate + bf_ref[...])
    tril = jnp.where(causal, 1.0, 0.0).astype(BF16)
    carry = carry_ref[...]
    cums = []
    for c in range(n_chunks):
        pieces = _split_bf16(log_f[c * CHUNK:(c + 1) * CHUNK])
        cum3 = _dot(tril, jnp.concatenate(pieces, axis=1))
        cum = carry
        for i in range(N_SPLIT):
            cum = cum + cum3[:, i * LANES:(i + 1) * LANES]
        carry = cum[CHUNK - 1:CHUNK, :]
        cums.append(cum)
    carry_ref[...] = carry
    c_all = jnp.concatenate(cums, axis=0)

    z_parts[o_half] = _dot(h, wall_ref[:, o_half:O_SPLIT])

    vn = (v * lax.rsqrt(ms + RMS_EPS) * gv_ref[...]).astype(BF16)
    w_pairs = []
    for p in range(N_GMLP_GROUPS // 2):
        w0 = jnp.where(causal, ws_ref[2 * p], 0.0).astype(BF16)
        w1 = jnp.where(causal, ws_ref[2 * p + 1], 0.0).astype(BF16)
        w_pairs.append(jnp.concatenate([w0, w1], axis=0))
    for cp in range(n_chunks // 2):
        rows0 = slice(2 * cp * CHUNK, (2 * cp + 1) * CHUNK)
        rows1 = slice((2 * cp + 1) * CHUNK, (2 * cp + 2) * CHUNK)
        parts0, parts1 = [], []
        for p in range(N_GMLP_GROUPS // 2):
            cols = slice(p * LANES, (p + 1) * LANES)
            r = _dot(w_pairs[p], jnp.concatenate([vn[rows0, cols], vn[rows1, cols]], axis=1))
            parts0.append(jnp.where(lane < HEAD_DIM, r[:CHUNK, :LANES], r[CHUNK:, :LANES]))
            parts1.append(jnp.where(lane < HEAD_DIM, r[:CHUNK, LANES:], r[CHUNK:, LANES:]))
        oa_ref[0, rows0, :] = (u[rows0] * (jnp.concatenate(parts0, axis=1) + bs_ref[...])).astype(BF16)
        oa_ref[0, rows1, :] = (u[rows1] * (jnp.concatenate(parts1, axis=1) + bs_ref[...])).astype(BF16)

    for p in range(N_MEM_HEADS // 2):
        vmp = kv[:, D_MEMQ + p * LANES:D_MEMQ + (p + 1) * LANES]
        outs = []
        for e in range(2):
            s = mem_scores[2 * p + e]
            pe = jnp.exp(s - jnp.max(s, axis=-1, keepdims=True))
            den = jnp.sum(pe, axis=-1, keepdims=True)
            outs.append(_dot(pe.astype(BF16), vmp) / den)
        oc_ref[0, :, p * LANES:(p + 1) * LANES] = jnp.where(lo_half_t, outs[0], outs[1]).astype(BF16)

    pieces = _split_bf16(jnp.where(lane_t < N_FOX_HEADS, c_all * LOG2E, 0.0))
    packed = pieces[0].astype(F32)
    for i in range(1, N_SPLIT):
        packed = packed + pltpu.roll(pieces[i].astype(F32), PIECE_STRIDE * i, axis=1)
    extras = _dot(packed.astype(BF16), place_ref[...])
    neg_one = jnp.full((tm, LANES), -1.0, F32)
    one = jnp.ones((tm, LANES), F32)
    zero = jnp.zeros((tm, LANES), F32)
    q_fill_even = jnp.where((lane_t >= HEAD_DIM) & (lane_t < HEAD_DIM + N_SPLIT), neg_one, zero)
    q_fill_odd = jnp.where(lane_t < N_SPLIT, neg_one, zero)
    v_fill_even = jnp.where(lane_t == HEAD_DIM, one, zero)
    v_fill_odd = jnp.where(lane_t == 0, one, zero)
    for p in range(N_FOX_HEADS // 2):
        cols = slice(p * LANES, (p + 1) * LANES)
        zq = fox_cols(p * LANES) * (QK_SCALE * LOG2E)
        zk = fox_cols(D_FOX + p * LANES)
        zv = fox_cols(2 * D_FOX + p * LANES)
        ex = extras[:, cols]
        q_ref[0, 2 * p] = jnp.where(lo_half_t, zq, q_fill_even).astype(BF16)
        q_ref[0, 2 * p + 1] = jnp.where(lo_half_t, q_fill_odd, zq).astype(BF16)
        k_ref[0, 2 * p] = jnp.where(lo_half_t, zk, ex).astype(BF16)
        k_ref[0, 2 * p + 1] = jnp.where(lo_half_t, ex, zk).astype(BF16)
        v_ref[0, 2 * p] = jnp.where(lo_half_t, zv, v_fill_even).T.astype(BF16)
        v_ref[0, 2 * p + 1] = jnp.where(lo_half_t, v_fill_odd, zv).T.astype(BF16)


def _placement_matrix():
    assert N_FOX_HEADS <= PIECE_STRIDE and N_SPLIT * PIECE_STRIDE <= LANES
    m = np.zeros((LANES, D_FOX), np.float32)
    for h in range(N_FOX_HEADS):
        base = (h // 2) * LANES + (HEAD_DIM if h % 2 == 0 else 0)
        for i in range(N_SPLIT):
            m[i * PIECE_STRIDE + h, base + i] = 1.0
    return m


def _group_mean_matrix():
    g = np.arange(D_GMLP) // HEAD_DIM
    return (g[:, None] == g[None, :]).astype(np.float32) / HEAD_DIM


def _mix_proj(x, layer, kv_all, p):
    b, s, d = x.shape
    tm = min(TM_PROJ, s)
    assert s % tm == 0 and tm % CHUNK == 0
    m = kv_all.shape[2]
    const2 = lambda shape: pl.BlockSpec(shape, lambda i, j: (0, 0))
    per_layer = lambda *shape: pl.BlockSpec((None,) + shape, lambda i, j: (layer,) + (0,) * len(shape))
    head_spec = pl.BlockSpec((1, N_FOX_HEADS, tm, LANES), lambda i, j: (i, 0, j, 0))
    head_shape = jax.ShapeDtypeStruct((b, N_FOX_HEADS, s, LANES), BF16)
    return pl.pallas_call(
        functools.partial(_mix_proj_kernel, tm=tm),
        grid=(b, s // tm),
        in_specs=[
            pl.BlockSpec((1, tm, d), lambda i, j: (i, j, 0)),
            per_layer(1, d),
            per_layer(d, W_ALL),
            per_layer(1, LANES),
            per_layer(1, D_GMLP),
            per_layer(N_GMLP_GROUPS, CHUNK, CHUNK),
            per_layer(CHUNK, D_GMLP),
            pl.BlockSpec((1, 1, m, 2 * D_MEMQ), lambda i, j: (layer, i, 0, 0)),
            const2((D_GMLP, D_GMLP)),
            const2((LANES, D_FOX)),
        ],
        out_specs=[
            pl.BlockSpec((1, tm, D_GMLP), lambda i, j: (i, j, 0)),
            pl.BlockSpec((1, tm, D_MEMQ), lambda i, j: (i, j, 0)),
            head_spec, head_spec,
            pl.BlockSpec((1, N_FOX_HEADS, LANES, tm), lambda i, j: (i, 0, 0, j)),
        ],
        out_shape=[
            jax.ShapeDtypeStruct((b, s, D_GMLP), BF16),
            jax.ShapeDtypeStruct((b, s, D_MEMQ), BF16),
            head_shape, head_shape,
            jax.ShapeDtypeStruct((b, N_FOX_HEADS, LANES, s), BF16),
        ],
        scratch_shapes=[pltpu.VMEM((1, LANES), F32)],
        compiler_params=pltpu.CompilerParams(
            dimension_semantics=("arbitrary", "arbitrary"), vmem_limit_bytes=VMEM_LIMIT),
        name="mix_proj",
    )(x, p["g_pre_mix"], p["w_all"], p["b_forget"], p["g_v"], p["w_s"], p["b_s"], kv_all, p["gmean"], p["place"])


def _fox_attn_kernel(q_ref, k_ref, vt_ref, o_ref, s0_ref, s1_ref, m_ref, acc_ref, *, tq):
    hps = q_ref.shape[1]
    n_q = q_ref.shape[2] // tq
    sub = lax.broadcasted_iota(jnp.int32, (LANES, tq), 0)
    bufs = (s0_ref, s1_ref)

    def scores(i, j, dst_ref):
        start = pl.multiple_of(j * tq, tq)
        for hh in range(hps):
            dst_ref[hh] = _dot_nt(k_ref[0, hh, pl.ds(start, tq), :], q_ref[0, hh, i * tq:(i + 1) * tq, :])

    def consume(j, src_ref):
        start = pl.multiple_of(j * tq, tq)
        for hh in range(hps):
            st = src_ref[hh]
            m = m_ref[hh]
            m_new = jnp.maximum(m, jnp.max(st, axis=0, keepdims=True))
            alpha = jnp.exp2(m - m_new)
            pt = jnp.exp2(st - m_new).astype(BF16)
            acc_ref[hh] = alpha * acc_ref[hh] + _dot(vt_ref[0, hh, :, pl.ds(start, tq)], pt)
            m_ref[hh] = m_new

    h = tq // 2
    tri = lax.broadcasted_iota(jnp.int32, (h, h), 0) <= lax.broadcasted_iota(jnp.int32, (h, h), 1)

    def scores_diag(i, dst_ref):
        base = i * tq
        for hh in range(hps):
            dst_ref[hh, :h, :] = _dot_nt(k_ref[0, hh, base:base + h, :], q_ref[0, hh, base:base + tq, :])
            dst_ref[hh, h:, h:] = _dot_nt(k_ref[0, hh, base + h:base + tq, :], q_ref[0, hh, base + h:base + tq, :])

    def consume_diag(i, src_ref):
        base = i * tq
        for hh in range(hps):
            s_ll = jnp.where(tri, src_ref[hh, :h, :h], NEG_INF)
            s_lh = src_ref[hh, :h, h:]
            s_hh = jnp.where(tri, src_ref[hh, h:, h:], NEG_INF)
            m_lo, m_hi = m_ref[hh, :, :h], m_ref[hh, :, h:]
            new_lo = jnp.maximum(m_lo, jnp.max(s_ll, axis=0, keepdims=True))
            new_hi = jnp.maximum(m_hi, jnp.maximum(jnp.max(s_lh, axis=0, keepdims=True),
                                                   jnp.max(s_hh, axis=0, keepdims=True)))
            p_ll = jnp.exp2(s_ll - new_lo).astype(BF16)
            p_hi = jnp.concatenate([jnp.exp2(s_lh - new_hi), jnp.exp2(s_hh - new_hi)], axis=0).astype(BF16)
            acc_ref[hh, :, :h] = (jnp.exp2(m_lo - new_lo) * acc_ref[hh, :, :h]
                                  + _dot(vt_ref[0, hh, :, base:base + h], p_ll))
            acc_ref[hh, :, h:] = (jnp.exp2(m_hi - new_hi) * acc_ref[hh, :, h:]
                                  + _dot(vt_ref[0, hh, :, base:base + tq], p_hi))
            m_ref[hh, :, :h] = new_lo
            m_ref[hh, :, h:] = new_hi

    def finish_tile(i):
        for p in range(hps // 2):
            outs = []
            for e in range(2):
                acc = acc_ref[2 * p + e]
                den_row = HEAD_DIM if e == 0 else 0
                outs.append(acc / acc[den_row:den_row + 1, :])
            pair_t = jnp.where(sub < HEAD_DIM, outs[0], outs[1])
            o_ref[0, i * tq:(i + 1) * tq, p * LANES:(p + 1) * LANES] = pair_t.T.astype(BF16)
        m_ref[...] = jnp.full(m_ref.shape, NEG_INF, F32)
        acc_ref[...] = jnp.zeros(acc_ref.shape, F32)

    m_ref[...] = jnp.full(m_ref.shape, NEG_INF, F32)
    acc_ref[...] = jnp.zeros(acc_ref.shape, F32)
    cur = 0
    scores_diag(0, bufs[0])
    for i in range(n_q):
        x_ref, y_ref = bufs[cur], bufs[1 - cur]

        def two_blocks(p, _, i=i, x_ref=x_ref, y_ref=y_ref):
            scores(i, 2 * p + 1, y_ref)
            consume(2 * p, x_ref)
            scores(i, 2 * p + 2, x_ref)
            consume(2 * p + 1, y_ref)
            return 0

        if i % 2 == 0:
            if i >= 2:
                if i >= 4:
                    lax.fori_loop(0, i // 2 - 1, two_blocks, 0)
                scores(i, i - 1, y_ref)
                consume(i - 2, x_ref)
                scores_diag(i, x_ref)
                consume(i - 1, y_ref)
            if i + 1 < n_q:
                scores(i + 1, 0, y_ref)
            consume_diag(i, x_ref)
            cur = 1 - cur
        else:
            if i >= 3:
                lax.fori_loop(0, i // 2, two_blocks, 0)
            scores_diag(i, y_ref)
            consume(i - 1, x_ref)
            if i + 1 < n_q:
                scores(i + 1, 0, x_ref)
            consume_diag(i, y_ref)
        finish_tile(i)


def _fox_attn(q, k, v):
    b, nh, s, _ = q.shape
    tq = min(TQ, s)
    hps = HEADS_PER_STEP
    assert s % tq == 0 and nh % hps == 0 and hps % 2 == 0
    qk_spec = pl.BlockSpec((1, hps, s, LANES), lambda i, p: (i, p, 0, 0))
    vt_spec = pl.BlockSpec((1, hps, LANES, s), lambda i, p: (i, p, 0, 0))
    return pl.pallas_call(
        functools.partial(_fox_attn_kernel, tq=tq),
        grid=(b, nh // hps),
        in_specs=[qk_spec, qk_spec, vt_spec],
        out_specs=pl.BlockSpec((1, s, hps * HEAD_DIM), lambda i, p: (i, 0, p)),
        out_shape=jax.ShapeDtypeStruct((b, s, nh * HEAD_DIM), BF16),
        scratch_shapes=[
            pltpu.VMEM((hps, tq, tq), F32), pltpu.VMEM((hps, tq, tq), F32),
            pltpu.VMEM((hps, 1, tq), F32), pltpu.VMEM((hps, LANES, tq), F32),
        ],
        compiler_params=pltpu.CompilerParams(
            dimension_semantics=("arbitrary", "arbitrary"), vmem_limit_bytes=VMEM_LIMIT),
        name="fox_attn",
    )(q, k, v)


def _out_ffn_kernel(x_ref, a_ref, b_ref, c_ref, wo_ref, g1_ref, g2_ref, g3_ref,
                    w1_ref, w2_ref, o_ref, *, ff_chunk, n_sub):
    tm = x_ref.shape[0]
    sub = tm // n_sub
    d_ff = w1_ref.shape[1]
    ys = []
    for s in range(n_sub):
        rows = slice(s * sub, (s + 1) * sub)
        mixed = jnp.concatenate([a_ref[rows, :], b_ref[rows, :], c_ref[rows, :]], axis=1)
        ys.append(_dot(mixed, wo_ref[...]))
    for s in range(n_sub):
        rows = slice(s * sub, (s + 1) * sub)
        x1 = x_ref[rows, :] + _rmsnorm(ys[s], g1_ref[...])
        h = _rmsnorm(x1, g2_ref[...]).astype(BF16)
        f = None
        for c in range(d_ff // ff_chunk):
            cols = slice(c * ff_chunk, (c + 1) * ff_chunk)
            a1 = jnp.maximum(_dot(h, w1_ref[:, cols]), 0.0)
            part = _dot((a1 * a1).astype(BF16), w2_ref[cols, :])
            f = part if f is None else f + part
        o_ref[rows, :] = x1 + _rmsnorm(f, g3_ref[...])


def _out_ffn(x, a, b_, c, layer, p):
    t, d = x.shape
    tm = min(TM_FFN, t)
    assert t % tm == 0
    d_ff = p["w_ff1"].shape[2]
    ff_chunk = min(FF_CHUNK, d_ff)
    row = lambda w: pl.BlockSpec((tm, w), lambda i: (i, 0))
    per_layer = lambda shape: pl.BlockSpec(
        (None,) + shape, lambda i: (layer,) + (0,) * len(shape), pipeline_mode=pl.Buffered(1))
    return pl.pallas_call(
        functools.partial(_out_ffn_kernel, ff_chunk=ff_chunk, n_sub=max(1, tm // FFN_SUB_ROWS)),
        grid=(t // tm,),
        in_specs=[
            row(d), row(D_GMLP), row(D_FOX), row(D_MEMQ),
            per_layer((D_GMLP + D_FOX + D_MEMQ, d)),
            per_layer((1, d)), per_layer((1, d)), per_layer((1, d)),
            per_layer((d, d_ff)), per_layer((d_ff, d)),
        ],
        out_specs=row(d),
        out_shape=jax.ShapeDtypeStruct((t, d), F32),
        compiler_params=pltpu.CompilerParams(
            dimension_semantics=("arbitrary",), vmem_limit_bytes=VMEM_LIMIT),
        name="out_ffn",
    )(x, a, b_, c, p["w_out"], p["g_post_mix"], p["g_pre_ffn"], p["g_post_ffn"],
      p["w_ff1"], p["w_ff2"])


def _pack_params(norm_pre_mix, norm_post_mix, norm_pre_ffn, norm_post_ffn, w_in, b_forget,
                 gmlp_v_norm, gmlp_w_s, gmlp_b_s, w_out, w_ff1, w_ff2):
    depth, d, _ = w_in.shape
    i_mq = O_GATE + N_FOX_HEADS
    pad = LANES - N_FOX_HEADS
    w_all = jnp.concatenate(
        [w_in[:, :, :i_mq].astype(BF16), jnp.zeros((depth, d, pad), BF16), w_in[:, :, i_mq:].astype(BF16)], axis=2)
    return {
        "g_pre_mix": norm_pre_mix.reshape(depth, 1, d),
        "w_all": w_all,
        "b_forget": jnp.pad(b_forget, ((0, 0), (0, pad))).reshape(depth, 1, LANES),
        "g_v": gmlp_v_norm.reshape(depth, 1, D_GMLP),
        "w_s": gmlp_w_s,
        "b_s": jnp.repeat(jnp.swapaxes(gmlp_b_s, 1, 2), HEAD_DIM, axis=2),
        "gmean": jnp.asarray(_group_mean_matrix(), BF16),
        "place": jnp.asarray(_placement_matrix(), BF16),
        "w_out": w_out.astype(BF16),
        "g_post_mix": norm_post_mix.reshape(depth, 1, d),
        "g_pre_ffn": norm_pre_ffn.reshape(depth, 1, d),
        "g_post_ffn": norm_post_ffn.reshape(depth, 1, d),
        "w_ff1": w_ff1.astype(BF16),
        "w_ff2": w_ff2.astype(BF16),
    }


def kernel(x, mem, norm_pre_mix, norm_post_mix, norm_pre_ffn, norm_post_ffn, norm_mem, w_in, b_forget,
           gmlp_v_norm, gmlp_w_s, gmlp_b_s, w_mem_kv, w_out, w_ff1, w_ff2):
    b, s, d = x.shape
    depth = w_in.shape[0]
    p = _pack_params(norm_pre_mix, norm_post_mix, norm_pre_ffn, norm_post_ffn, w_in, b_forget,
                     gmlp_v_norm, gmlp_w_s, gmlp_b_s, w_out, w_ff1, w_ff2)
    kv_all = _memkv(mem, norm_mem, w_mem_kv.astype(BF16))
    for l in range(depth):
        out_a, out_c, q, k, v = _mix_proj(x, l, kv_all, p)
        out_b = _fox_attn(q, k, v)
        x = _out_ffn(x.reshape(b * s, d), out_a.reshape(b * s, D_GMLP), out_b.reshape(b * s, D_FOX),
                     out_c.reshape(b * s, D_MEMQ), l, p).reshape(b, s, d)
    return x
```

```python
import functools
import math

import numpy as np
import jax
import jax.numpy as jnp
from jax import lax
from jax.experimental import pallas as pl
from jax.experimental.pallas import tpu as pltpu

F32 = jnp.float32
BF16 = jnp.bfloat16

HEAD_DIM = 64
LANES = 128
N_GMLP_GROUPS = 6
N_FOX_HEADS = 6
N_MEM_HEADS = 4
D_GMLP = N_GMLP_GROUPS * HEAD_DIM
D_FOX = N_FOX_HEADS * HEAD_DIM
D_MEMQ = N_MEM_HEADS * HEAD_DIM
CHUNK = 128
RMS_EPS = 1e-6
NEG_INF = -1e30
QK_SCALE = HEAD_DIM ** -0.5
LOG2E = math.log2(math.e)
N_SPLIT = 3
PIECE_STRIDE = 8
O_FX = 2 * D_GMLP
O_GATE = O_FX + 3 * D_FOX
O_MQ = O_GATE + LANES
W_ALL = O_MQ + D_MEMQ
MXU_TILE = 256
O_SPLIT = O_GATE - (O_GATE - O_FX) % MXU_TILE
assert O_FX % MXU_TILE == 0 and (O_SPLIT - O_FX) % MXU_TILE == 0 and (W_ALL - O_SPLIT) % MXU_TILE == 0

V7X_VMEM_BYTES = 64 * 1024 * 1024
VMEM_LIMIT = V7X_VMEM_BYTES - 8 * 1024 * 1024

TM_PROJ = 1024
TQ = 512
HEADS_PER_STEP = 2
TM_FFN = 1024
FFN_SUB_ROWS = 256
FF_CHUNK = 1024


def _dot(a, b):
    return jnp.dot(a, b, preferred_element_type=F32)


def _dot_nt(a, b):
    return lax.dot_general(a, b, (((1,), (1,)), ((), ())), preferred_element_type=F32)


def _rmsnorm(x, g):
    return x * lax.rsqrt(jnp.mean(x * x, axis=-1, keepdims=True) + RMS_EPS) * g


def _gelu_tanh(x):
    c = math.sqrt(2.0 / math.pi)
    return 0.5 * x * (1.0 + jnp.tanh(c * (x + 0.044715 * (x * x * x))))


def _log_sigmoid(x):
    return -(jnp.maximum(-x, 0.0) + jnp.log1p(jnp.exp(-jnp.abs(x))))


def _split_bf16(x):
    pieces = []
    r = x
    for _ in range(N_SPLIT):
        p = r.astype(BF16)
        pieces.append(p)
        r = r - p.astype(F32)
    return pieces


def _memkv_kernel(mem_ref, g_ref, w_ref, o_ref):
    mn = _rmsnorm(mem_ref[...], g_ref[...]).astype(BF16)
    o_ref[...] = _dot(mn, w_ref[...]).astype(BF16)


def _memkv(mem, norm_mem, w_kv_bf16):
    depth = w_kv_bf16.shape[0]
    b, m, d = mem.shape
    n = w_kv_bf16.shape[-1]
    kv = pl.pallas_call(
        _memkv_kernel,
        grid=(depth,),
        in_specs=[
            pl.BlockSpec((b * m, d), lambda l: (0, 0)),
            pl.BlockSpec((None, 1, d), lambda l: (l, 0, 0)),
            pl.BlockSpec((None, d, n), lambda l: (l, 0, 0)),
        ],
        out_specs=pl.BlockSpec((None, b * m, n), lambda l: (l, 0, 0)),
        out_shape=jax.ShapeDtypeStruct((depth, b * m, n), BF16),
        compiler_params=pltpu.CompilerParams(dimension_semantics=("arbitrary",), vmem_limit_bytes=VMEM_LIMIT),
        name="mem_kv",
    )(mem.reshape(b * m, d), norm_mem.reshape(depth, 1, d), w_kv_bf16)
    return kv.reshape(depth, b, m, n)


def _mix_proj_kernel(x_ref, gpre_ref, wall_ref, bf_ref, gv_ref, ws_ref, bs_ref, kv_ref, gmean_ref, place_ref,
                     oa_ref, oc_ref, q_ref, k_ref, v_ref, carry_ref, *, tm):
    n_chunks = tm // CHUNK
    assert n_chunks % 2 == 0

    @pl.when(pl.program_id(1) == 0)
    def _():
        carry_ref[...] = jnp.zeros_like(carry_ref)

    lane =lax.broadcasted_iota(jnp.int32, (CHUNK, LANES), 1)
    lane_t = lax.broadcasted_iota(jnp.int32, (tm, LANES), 1)
    lo_half_t = lane_t < HEAD_DIM
    row = lax.broadcasted_iota(jnp.int32, (CHUNK, CHUNK), 0)
    col = lax.broadcasted_iota(jnp.int32, (CHUNK, CHUNK), 1)
    causal = row >= col

    h = _rmsnorm(x_ref[0], gpre_ref[...]).astype(BF16)
    zg = _dot(h, wall_ref[:, :O_FX])
    z_tail = _dot(h, wall_ref[:, O_SPLIT:])
    o_half = O_FX + (O_SPLIT - O_FX) // 2
    assert (o_half - O_FX) % MXU_TILE == 0
    z_parts = {O_FX: _dot(h, wall_ref[:, O_FX:o_half]), O_SPLIT: z_tail}
    zgate = z_tail[:, O_GATE - O_SPLIT:O_MQ - O_SPLIT]
    zm = z_tail[:, O_MQ - O_SPLIT:] * QK_SCALE

    def fox_cols(start):
        col = O_FX + start
        base = max(b for b in z_parts if b <= col)
        return z_parts[base][:, col - base:col - base + LANES]

    u = _gelu_tanh(zg[:, :D_GMLP])
    v = _gelu_tanh(zg[:, D_GMLP:])
    sq = (v * v).astype(BF16)
    split = (D_GMLP // MXU_TILE) * MXU_TILE
    ms = jnp.concatenate([_dot(sq[:, :split], gmean_ref[:split, :split]),
                          _dot(sq[:, split:], gmean_ref[split:, split:])], axis=1)

    kv = kv_ref[0, 0]
    mem_scores = []
    for hd in range(N_MEM_HEADS):
        p, e = divmod(hd, 2)
        keep = lo_half_t if e == 0 else jnp.logical_not(lo_half_t)
        qh = jnp.where(keep, zm[:, p * LANES:(p + 1) * LANES], 0.0).astype(BF16)
        mem_scores.append(_dot_nt(qh, kv[:, p * LANES:(p + 1) * LANES]))

    log_f = _log_sigmoid(zgate + bf_ref[...])
    tril = jnp.where(causal, 1.0, 0.0).astype(BF16)
    carry = carry_ref[...]
    cums = []
    for c in range(n_chunks):
        pieces = _split_bf16(log_f[c * CHUNK:(c + 1) * CHUNK])
        cum3 = _dot(tril, jnp.concatenate(pieces, axis=1))
        cum = carry
        for i in range(N_SPLIT):
            cum = cum + cum3[:, i * LANES:(i + 1) * LANES]
        carry = cum[CHUNK - 1:CHUNK, :]
        cums.append(cum)
    carry_ref[...] = carry
    c_all = jnp.concatenate(cums, axis=0)

    z_parts[o_half] = _dot(h, wall_ref[:, o_half:O_SPLIT])

    vn = (v * lax.rsqrt(ms + RMS_EPS) * gv_ref[...]).astype(BF16)
    w_pairs = []
    for p in range(N_GMLP_GROUPS // 2):
        w0 = jnp.where(causal, ws_ref[2 * p], 0.0).astype(BF16)
        w1 = jnp.where(causal, ws_ref[2 * p + 1], 0.0).astype(BF16)
        w_pairs.append(jnp.concatenate([w0, w1], axis=0))
    for cp in range(n_chunks // 2):
        rows0 = slice(2 * cp * CHUNK, (2 * cp + 1) * CHUNK)
        rows1 = slice((2 * cp + 1) * CHUNK, (2 * cp + 2) * CHUNK)
        parts0, parts1 = [], []
        for p in range(N_GMLP_GROUPS // 2):
            cols = slice(p * LANES, (p + 1) * LANES)
            r = _dot(w_pairs[p], jnp.concatenate([vn[rows0, cols], vn[rows1, cols]], axis=1))
            parts0.append(jnp.where(lane < HEAD_DIM, r[:CHUNK, :LANES], r[CHUNK:, :LANES]))
            parts1.append(jnp.where(lane < HEAD_DIM, r[:CHUNK, LANES:], r[CHUNK:, LANES:]))
        oa_ref[0, rows0, :] = (u[rows0] * (jnp.concatenate(parts0, axis=1) + bs_ref[...])).astype(BF16)
        oa_ref[0, rows1, :] = (u[rows1] * (jnp.concatenate(parts1, axis=1) + bs_ref[...])).astype(BF16)

    for p in range(N_MEM_HEADS // 2):
        vmp = kv[:, D_MEMQ + p * LANES:D_MEMQ + (p + 1) * LANES]
        outs = []
        for e in range(2):
            s = mem_scores[2 * p + e]
            pe = jnp.exp(s - jnp.max(s, axis=-1, keepdims=True))
            den = jnp.sum(pe, axis=-1, keepdims=True)
            outs.append(_dot(pe.astype(BF16), vmp) / den)
        oc_ref[0, :, p * LANES:(p + 1) * LANES] = jnp.where(lo_half_t, outs[0], outs[1]).astype(BF16)

    pieces = _split_bf16(jnp.where(lane_t < N_FOX_HEADS, c_all * LOG2E, 0.0))
    packed = pieces[0].astype(F32)
    for i in range(1, N_SPLIT):
        packed = packed + pltpu.roll(pieces[i].astype(F32), PIECE_STRIDE * i, axis=1)
    extras = _dot(packed.astype(BF16), place_ref[...])
    neg_one = jnp.full((tm, LANES), -1.0, F32)
    one = jnp.ones((tm, LANES), F32)
    zero = jnp.zeros((tm, LANES), F32)
    q_fill_even = jnp.where((lane_t >= HEAD_DIM) & (lane_t < HEAD_DIM + N_SPLIT), neg_one, zero)
    q_fill_odd = jnp.where(lane_t < N_SPLIT, neg_one, zero)
    v_fill_even = jnp.where(lane_t == HEAD_DIM, one, zero)
    v_fill_odd = jnp.where(lane_t == 0, one, zero)
    for p in range(N_FOX_HEADS // 2):
        cols = slice(p * LANES, (p + 1) * LANES)
        zq = fox_cols(p * LANES) * (QK_SCALE * LOG2E)
        zk = fox_cols(D_FOX + p * LANES)
        zv = fox_cols(2 * D_FOX + p * LANES)
        ex = extras[:, cols]
        q_ref[0, 2 * p] = jnp.where(lo_half_t, zq, q_fill_even).astype(BF16)
        q_ref[0, 2 * p + 1] = jnp.where(lo_half_t, q_fill_odd, zq).astype(BF16)
        k_ref[0, 2 * p] = jnp.where(lo_half_t, zk, ex).astype(BF16)
        k_ref[0, 2 * p + 1] = jnp.where(lo_half_t, ex, zk).astype(BF16)
        v_ref[0, 2 * p] = jnp.where(lo_half_t, zv, v_fill_even).T.astype(BF16)
        v_ref[0, 2 * p + 1] = jnp.where(lo_half_t, v_fill_odd, zv).T.astype(BF16)


def _placement_matrix():
    assert N_FOX_HEADS <= PIECE_STRIDE and N_SPLIT * PIECE_STRIDE <= LANES
    m = np.zeros((LANES, D_FOX), np.float32)
    for h in range(N_FOX_HEADS):
        base = (h // 2) * LANES + (HEAD_DIM if h % 2 == 0 else 0)
        for i in range(N_SPLIT):
            m[i * PIECE_STRIDE + h, base + i] = 1.0
    return m


def _group_mean_matrix():
    g = np.arange(D_GMLP) // HEAD_DIM
    return (g[:, None] == g[None, :]).astype(np.float32) / HEAD_DIM


def _mix_proj(x, layer, kv_all, p):
    b, s, d = x.shape
    tm = min(TM_PROJ, s)
    assert s % tm == 0 and tm % CHUNK == 0
    m = kv_all.shape[2]
    const2 = lambda shape: pl.BlockSpec(shape, lambda i, j: (0, 0))
    per_layer = lambda *shape: pl.BlockSpec((None,) + shape, lambda i, j: (layer,) + (0,) * len(shape))
    head_spec = pl.BlockSpec((1, N_FOX_HEADS, tm, LANES), lambda i, j: (i, 0, j, 0))
    head_shape = jax.ShapeDtypeStruct((b, N_FOX_HEADS, s, LANES), BF16)
    return pl.pallas_call(
        functools.partial(_mix_proj_kernel, tm=tm),
        grid=(b, s // tm),
        in_specs=[
            pl.BlockSpec((1, tm, d), lambda i, j: (i, j, 0)),
            per_layer(1, d),
            per_layer(d, W_ALL),
            per_layer(1, LANES),
            per_layer(1, D_GMLP),
            per_layer(N_GMLP_GROUPS, CHUNK, CHUNK),
            per_layer(CHUNK, D_GMLP),
            pl.BlockSpec((1, 1, m, 2 * D_MEMQ), lambda i, j: (layer, i, 0, 0)),
            const2((D_GMLP, D_GMLP)),
            const2((LANES, D_FOX)),
        ],
        out_specs=[
            pl.BlockSpec((1, tm, D_GMLP), lambda i, j: (i, j, 0)),
            pl.BlockSpec((1, tm, D_MEMQ), lambda i, j: (i, j, 0)),
            head_spec, head_spec,
            pl.BlockSpec((1, N_FOX_HEADS, LANES, tm), lambda i, j: (i, 0, 0, j)),
        ],
        out_shape=[
            jax.ShapeDtypeStruct((b, s, D_GMLP), BF16),
            jax.ShapeDtypeStruct((b, s, D_MEMQ), BF16),
            head_shape, head_shape,
            jax.ShapeDtypeStruct((b, N_FOX_HEADS, LANES, s), BF16),
        ],
        scratch_shapes=[pltpu.VMEM((1, LANES), F32)],
        compiler_params=pltpu.CompilerParams(
            dimension_semantics=("arbitrary", "arbitrary"), vmem_limit_bytes=VMEM_LIMIT),
        name="mix_proj",
    )(x, p["g_pre_mix"], p["w_all"], p["b_forget"], p["g_v"], p["w_s"], p["b_s"], kv_all, p["gmean"], p["place"])


def _fox_attn_kernel(q_ref, k_ref, vt_ref, o_ref, s0_ref, s1_ref, bm0_ref, bm1_ref, m_ref, acc_ref, *, tq):
    hps = q_ref.shape[1]
    n_q = q_ref.shape[2] // tq
    sub = lax.broadcasted_iota(jnp.int32, (LANES, tq), 0)
    bufs = ((s0_ref, bm0_ref), (s1_ref, bm1_ref))

    def scores(i, j, buf):
        dst_ref, bm_ref = buf
        start = pl.multiple_of(j * tq, tq)
        for hh in range(hps):
            st = _dot_nt(k_ref[0, hh, pl.ds(start, tq), :], q_ref[0, hh, i * tq:(i + 1) * tq, :])
            dst_ref[hh] = st
            bm_ref[hh] = jnp.max(st, axis=0, keepdims=True)

    def consume(j, buf):
        src_ref, bm_ref = buf
        start = pl.multiple_of(j * tq, tq)
        for hh in range(hps):
            m = m_ref[hh]
            m_new = jnp.maximum(m, bm_ref[hh])
            alpha = jnp.exp2(m - m_new)
            pt = jnp.exp2(src_ref[hh] - m_new).astype(BF16)
            acc_ref[hh] = alpha * acc_ref[hh] + _dot(vt_ref[0, hh, :, pl.ds(start, tq)], pt)
            m_ref[hh] = m_new

    h = tq // 2
    tri = lax.broadcasted_iota(jnp.int32, (h, h), 0) <= lax.broadcasted_iota(jnp.int32, (h, h), 1)

    def scores_diag(i, buf):
        dst_ref, bm_ref = buf
        base = i * tq
        for hh in range(hps):
            k_lo, k_hi = k_ref[0, hh, base:base + h, :], k_ref[0, hh, base + h:base + tq, :]
            q_lo, q_hi = q_ref[0, hh, base:base + h, :], q_ref[0, hh, base + h:base + tq, :]
            s_ll = jnp.where(tri, _dot_nt(k_lo, q_lo), NEG_INF)
            s_lh = _dot_nt(k_lo, q_hi)
            s_hh = jnp.where(tri, _dot_nt(k_hi, q_hi), NEG_INF)
            dst_ref[hh, :h, :h] = s_ll
            dst_ref[hh, :h, h:] = s_lh
            dst_ref[hh, h:, h:] = s_hh
            bm_ref[hh, :, :h] = jnp.max(s_ll, axis=0, keepdims=True)
            bm_ref[hh, :, h:] = jnp.maximum(jnp.max(s_lh, axis=0, keepdims=True), jnp.max(s_hh, axis=0, keepdims=True))

    def consume_diag(i, buf):
        src_ref, bm_ref = buf
        base = i * tq
        for hh in range(hps):
            s_ll, s_lh, s_hh = src_ref[hh, :h, :h], src_ref[hh, :h, h:], src_ref[hh, h:, h:]
            m_lo, m_hi = m_ref[hh, :, :h], m_ref[hh, :, h:]
            new_lo = jnp.maximum(m_lo, bm_ref[hh, :, :h])
            new_hi = jnp.maximum(m_hi, bm_ref[hh, :, h:])
            p_ll = jnp.exp2(s_ll - new_lo).astype(BF16)
            p_hi = jnp.concatenate([jnp.exp2(s_lh - new_hi), jnp.exp2(s_hh - new_hi)], axis=0).astype(BF16)
            acc_ref[hh, :, :h] = (jnp.exp2(m_lo - new_lo) * acc_ref[hh, :, :h]
                                  + _dot(vt_ref[0, hh, :, base:base + h], p_ll))
            acc_ref[hh, :, h:] = (jnp.exp2(m_hi - new_hi) * acc_ref[hh, :, h:]
                                  + _dot(vt_ref[0, hh, :, base:base + tq], p_hi))
            m_ref[hh, :, :h] = new_lo
            m_ref[hh, :, h:] = new_hi

    def finish_tile(i):
        for p in range(hps // 2):
            outs = []
            for e in range(2):
                acc = acc_ref[2 * p + e]
                den_row = HEAD_DIM if e == 0 else 0
                outs.append(acc / acc[den_row:den_row + 1, :])
            pair_t = jnp.where(sub < HEAD_DIM, outs[0], outs[1])
            o_ref[0, i * tq:(i + 1) * tq, p * LANES:(p + 1) * LANES] = pair_t.T.astype(BF16)
        m_ref[...] = jnp.full(m_ref.shape, NEG_INF, F32)
        acc_ref[...] = jnp.zeros(acc_ref.shape, F32)

    m_ref[...] = jnp.full(m_ref.shape, NEG_INF, F32)
    acc_ref[...] = jnp.zeros(acc_ref.shape, F32)
    cur = 0
    scores_diag(0, bufs[0])
    for i in range(n_q):
        x_ref, y_ref = bufs[cur], bufs[1 - cur]

        def two_blocks(p, _, i=i, x_ref=x_ref, y_ref=y_ref):
            scores(i, 2 * p + 1, y_ref)
            consume(2 * p, x_ref)
            scores(i, 2 * p + 2, x_ref)
            consume(2 * p + 1, y_ref)
            return 0

        if i % 2 == 0:
            if i >= 2:
                if i >= 4:
                    lax.fori_loop(0, i // 2 - 1, two_blocks, 0)
                scores(i, i - 1, y_ref)
                consume(i - 2, x_ref)
                scores_diag(i, x_ref)
                consume(i - 1, y_ref)
            if i + 1 < n_q:
                scores(i + 1, 0, y_ref)
            consume_diag(i, x_ref)
            cur = 1 - cur
        else:
            if i >= 3:
                lax.fori_loop(0, i // 2, two_blocks, 0)
            scores_diag(i, y_ref)
            consume(i - 1, x_ref)
            if i + 1 < n_q:
                scores(i + 1, 0, x_ref)
            consume_diag(i, y_ref)
        finish_tile(i)


def _fox_attn(q, k, v):
    b, nh, s, _ = q.shape
    tq = min(TQ, s)
    hps = HEADS_PER_STEP
    assert s % tq == 0 and nh % hps == 0 and hps % 2 == 0
    qk_spec = pl.BlockSpec((1, hps, s, LANES), lambda i, p: (i, p, 0, 0))
    vt_spec = pl.BlockSpec((1, hps, LANES, s), lambda i, p: (i, p, 0, 0))
    return pl.pallas_call(
        functools.partial(_fox_attn_kernel, tq=tq),
        grid=(b, nh // hps),
        in_specs=[qk_spec, qk_spec, vt_spec],
        out_specs=pl.BlockSpec((1, s, hps * HEAD_DIM), lambda i, p: (i, 0, p)),
        out_shape=jax.ShapeDtypeStruct((b, s, nh * HEAD_DIM), BF16),
        scratch_shapes=[
            pltpu.VMEM((hps, tq, tq), F32), pltpu.VMEM((hps, tq, tq), F32),
            pltpu.VMEM((hps, 1, tq), F32), pltpu.VMEM((hps, 1, tq), F32),
            pltpu.VMEM((hps, 1, tq), F32), pltpu.VMEM((hps, LANES, tq), F32),
        ],
        compiler_params=pltpu.CompilerParams(
            dimension_semantics=("arbitrary", "arbitrary"), vmem_limit_bytes=VMEM_LIMIT),
        name="fox_attn",
    )(q, k, v)


def _out_ffn_kernel(x_ref, a_ref, b_ref, c_ref, wo_ref, g1_ref, g2_ref, g3_ref,
                    w1_ref, w2_ref, o_ref, *, ff_chunk, n_sub):
    tm = x_ref.shape[0]
    sub = tm // n_sub
    d_ff = w1_ref.shape[1]
    ys = []
    for s in range(n_sub):
        rows = slice(s * sub, (s + 1) * sub)
        mixed = jnp.concatenate([a_ref[rows, :], b_ref[rows, :], c_ref[rows, :]], axis=1)
        ys.append(_dot(mixed, wo_ref[...]))
    for s in range(n_sub):
        rows = slice(s * sub, (s + 1) * sub)
        x1 = x_ref[rows, :] + _rmsnorm(ys[s], g1_ref[...])
        h = _rmsnorm(x1, g2_ref[...]).astype(BF16)
        f = None
        for c in range(d_ff // ff_chunk):
            cols = slice(c * ff_chunk, (c + 1) * ff_chunk)
            a1 = jnp.maximum(_dot(h, w1_ref[:, cols]), 0.0)
            part = _dot((a1 * a1).astype(BF16), w2_ref[cols, :])
            f = part if f is None else f + part
        o_ref[rows, :] = x1 + _rmsnorm(f, g3_ref[...])


def _out_ffn(x, a, b_, c, layer, p):
    t, d = x.shape
    tm = min(TM_FFN, t)
    assert t % tm == 0
    d_ff = p["w_ff1"].shape[2]
    ff_chunk = min(FF_CHUNK, d_ff)
    row = lambda w: pl.BlockSpec((tm, w), lambda i: (i, 0))
    per_layer = lambda shape: pl.BlockSpec(
        (None,) + shape, lambda i: (layer,) + (0,) * len(shape), pipeline_mode=pl.Buffered(1))
    return pl.pallas_call(
        functools.partial(_out_ffn_kernel, ff_chunk=ff_chunk, n_sub=max(1, tm // FFN_SUB_ROWS)),
        grid=(t // tm,),
        in_specs=[
            row(d), row(D_GMLP), row(D_FOX), row(D_MEMQ),
            per_layer((D_GMLP + D_FOX + D_MEMQ, d)),
            per_layer((1, d)), per_layer((1, d)), per_layer((1, d)),
            per_layer((d, d_ff)), per_layer((d_ff, d)),
        ],
        out_specs=row(d),
        out_shape=jax.ShapeDtypeStruct((t, d), F32),
        compiler_params=pltpu.CompilerParams(
            dimension_semantics=("arbitrary",), vmem_limit_bytes=VMEM_LIMIT),
        name="out_ffn",
    )(x, a, b_, c, p["w_out"], p["g_post_mix"], p["g_pre_ffn"], p["g_post_ffn"],
      p["w_ff1"], p["w_ff2"])


def _pack_params(norm_pre_mix, norm_post_mix, norm_pre_ffn, norm_post_ffn, w_in, b_forget,
                 gmlp_v_norm, gmlp_w_s, gmlp_b_s, w_out, w_ff1, w_ff2):
    depth, d, _ = w_in.shape
    i_mq = O_GATE + N_FOX_HEADS
    pad = LANES - N_FOX_HEADS
    w_all = jnp.concatenate(
        [w_in[:, :, :i_mq].astype(BF16), jnp.zeros((depth, d, pad), BF16), w_in[:, :, i_mq:].astype(BF16)], axis=2)
    return {
        "g_pre_mix": norm_pre_mix.reshape(depth, 1, d),
        "w_all": w_all,
        "b_forget": jnp.pad(b_forget, ((0, 0), (0, pad))).reshape(depth, 1, LANES),
        "g_v": gmlp_v_norm.reshape(depth, 1, D_GMLP),
        "w_s": gmlp_w_s,
        "b_s": jnp.repeat(jnp.swapaxes(gmlp_b_s, 1, 2), HEAD_DIM, axis=2),
        "gmean": jnp.asarray(_group_mean_matrix(), BF16),
        "place": jnp.asarray(_placement_matrix(), BF16),
        "w_out": w_out.astype(BF16),
        "g_post_mix": norm_post_mix.reshape(depth, 1, d),
        "g_pre_ffn": norm_pre_ffn.reshape(depth, 1, d),
        "g_post_ffn": norm_post_ffn.reshape(depth, 1, d),
        "w_ff1": w_ff1.astype(BF16),
        "w_ff2": w_ff2.astype(BF16),
    }


def kernel(x, mem, norm_pre_mix, norm_post_mix, norm_pre_ffn, norm_post_ffn, norm_mem, w_in, b_forget,
           gmlp_v_norm, gmlp_w_s, gmlp_b_s, w_mem_kv, w_out, w_ff1, w_ff2):
    b, s, d = x.shape
    depth = w_in.shape[0]
    p = _pack_params(norm_pre_mix, norm_post_mix, norm_pre_ffn, norm_post_ffn, w_in, b_forget,
                     gmlp_v_norm, gmlp_w_s, gmlp_b_s, w_out, w_ff1, w_ff2)
    kv_all = _memkv(mem, norm_mem, w_mem_kv.astype(BF16))
    for l in range(depth):
        out_a, out_c, q, k, v = _mix_proj(x, l, kv_all, p)
        out_b = _fox_attn(q, k, v)
        x = _out_ffn(x.reshape(b * s, d), out_a.reshape(b * s, D_GMLP), out_b.reshape(b * s, D_FOX),
                     out_c.reshape(b * s, D_MEMQ), l, p).reshape(b, s, d)
    return x
```

```python
import functools
import math

import numpy as np
import jax
import jax.numpy as jnp
from jax import lax
from jax.experimental import pallas as pl
from jax.experimental.pallas import tpu as pltpu

F32 = jnp.float32
BF16 = jnp.bfloat16

HEAD_DIM = 64
LANES = 128
N_GMLP_GROUPS = 6
N_FOX_HEADS = 6
N_MEM_HEADS = 4
D_GMLP = N_GMLP_GROUPS * HEAD_DIM
D_FOX = N_FOX_HEADS * HEAD_DIM
D_MEMQ = N_MEM_HEADS * HEAD_DIM
CHUNK = 128
RMS_EPS = 1e-6
NEG_INF = -1e30
QK_SCALE = HEAD_DIM ** -0.5
LOG2E = math.log2(math.e)
N_SPLIT = 3
PIECE_STRIDE = 8
O_FX = 2 * D_GMLP
O_GATE = O_FX + 3 * D_FOX
O_MQ = O_GATE + LANES
W_ALL = O_MQ + D_MEMQ
MXU_TILE = 256
O_SPLIT = O_GATE - (O_GATE - O_FX) % MXU_TILE
assert O_FX % MXU_TILE == 0 and (O_SPLIT - O_FX) % MXU_TILE == 0 and (W_ALL - O_SPLIT) % MXU_TILE == 0

V7X_VMEM_BYTES = 64 * 1024 * 1024
VMEM_LIMIT = V7X_VMEM_BYTES - 8 * 1024 * 1024

TM_PROJ = 1024
TQ = 512
HEADS_PER_STEP = 2
TM_FFN = 1024
FFN_SUB_ROWS = 256
FF_CHUNK = 1024


def _dot(a, b):
    return jnp.dot(a, b, preferred_element_type=F32)


def _dot_nt(a, b):
    return lax.dot_general(a, b, (((1,), (1,)), ((), ())), preferred_element_type=F32)


def _rmsnorm(x, g):
    return x * lax.rsqrt(jnp.mean(x * x, axis=-1, keepdims=True) + RMS_EPS) * g


def _gelu_tanh(x):
    c = math.sqrt(2.0 / math.pi)
    return 0.5 * x * (1.0 + jnp.tanh(c * (x + 0.044715 * (x * x * x))))


def _log_sigmoid(x):
    return -(jnp.maximum(-x, 0.0) + jnp.log1p(jnp.exp(-jnp.abs(x))))


def _split_bf16(x):
    pieces = []
    r = x
    for _ in range(N_SPLIT):
        p = r.astype(BF16)
        pieces.append(p)
        r = r - p.astype(F32)
    return pieces


def _memkv_kernel(mem_ref, g_ref, w_ref, o_ref):
    mn = _rmsnorm(mem_ref[...], g_ref[...]).astype(BF16)
    o_ref[...] = _dot(mn, w_ref[...]).astype(BF16)


def _memkv(mem, norm_mem, w_kv_bf16):
    depth = w_kv_bf16.shape[0]
    b, m, d = mem.shape
    n = w_kv_bf16.shape[-1]
    kv = pl.pallas_call(
        _memkv_kernel,
        grid=(depth,),
        in_specs=[
            pl.BlockSpec((b * m, d), lambda l: (0, 0)),
            pl.BlockSpec((None, 1, d), lambda l: (l, 0, 0)),
            pl.BlockSpec((None, d, n), lambda l: (l, 0, 0)),
        ],
        out_specs=pl.BlockSpec((None, b * m, n), lambda l: (l, 0, 0)),
        out_shape=jax.ShapeDtypeStruct((depth, b * m, n), BF16),
        compiler_params=pltpu.CompilerParams(dimension_semantics=("arbitrary",), vmem_limit_bytes=VMEM_LIMIT),
        name="mem_kv",
    )(mem.reshape(b * m, d), norm_mem.reshape(depth, 1, d), w_kv_bf16)
    return kv.reshape(depth, b, m, n)


def _mix_proj_kernel(x_ref, gpre_ref, wall_ref, bf_ref, gv_ref, ws_ref, bs_ref, kv_ref, gmean_ref, place_ref,
                     oa_ref, oc_ref, q_ref, k_ref, v_ref, carry_ref, *, tm):
    n_chunks = tm // CHUNK
    assert n_chunks % 2 == 0

    @pl.when(pl.program_id(1) == 0)
    def _():
        carry_ref[...] = jnp.zeros_like(carry_ref)

    lane =lax.broadcasted_iota(jnp.int32, (CHUNK, LANES), 1)
    lane_t = lax.broadcasted_iota(jnp.int32, (tm, LANES), 1)
    lo_half_t = lane_t < HEAD_DIM
    row = lax.broadcasted_iota(jnp.int32, (CHUNK, CHUNK), 0)
    col = lax.broadcasted_iota(jnp.int32, (CHUNK, CHUNK), 1)
    causal = row >= col

    h = _rmsnorm(x_ref[0], gpre_ref[...]).astype(BF16)
    zg = _dot(h, wall_ref[:, :O_FX])
    z_tail = _dot(h, wall_ref[:, O_SPLIT:])
    o_half = O_FX + (O_SPLIT - O_FX) // 2
    assert (o_half - O_FX) % MXU_TILE == 0
    z_parts = {O_FX: _dot(h, wall_ref[:, O_FX:o_half]), O_SPLIT: z_tail}
    zgate = z_tail[:, O_GATE - O_SPLIT:O_MQ - O_SPLIT]
    zm = z_tail[:, O_MQ - O_SPLIT:] * QK_SCALE

    def fox_cols(start):
        col = O_FX + start
        base = max(b for b in z_parts if b <= col)
        return z_parts[base][:, col - base:col - base + LANES]

    u = _gelu_tanh(zg[:, :D_GMLP])
    v = _gelu_tanh(zg[:, D_GMLP:])
    sq = (v * v).astype(BF16)
    split = (D_GMLP // MXU_TILE) * MXU_TILE
    ms = jnp.concatenate([_dot(sq[:, :split], gmean_ref[:split, :split]),
                          _dot(sq[:, split:], gmean_ref[split:, split:])], axis=1)

    kv = kv_ref[0, 0]
    mem_scores = []
    for hd in range(N_MEM_HEADS):
        p, e = divmod(hd, 2)
        keep = lo_half_t if e == 0 else jnp.logical_not(lo_half_t)
        qh = jnp.where(keep, zm[:, p * LANES:(p + 1) * LANES], 0.0).astype(BF16)
        mem_scores.append(_dot_nt(qh, kv[:, p * LANES:(p + 1) * LANES]))

    log_f = _log_sigmoid(zgate + bf_ref[...])
    tril = jnp.where(causal, 1.0, 0.0).astype(BF16)
    carry = carry_ref[...]
    cums = []
    for c in range(n_chunks):
        pieces = _split_bf16(log_f[c * CHUNK:(c + 1) * CHUNK])
        cum3 = _dot(tril, jnp.concatenate(pieces, axis=1))
        cum = carry
        for i in range(N_SPLIT):
            cum = cum + cum3[:, i * LANES:(i + 1) * LANES]
        carry = cum[CHUNK - 1:CHUNK, :]
        cums.append(cum)
    carry_ref[...] = carry
    c_all = jnp.concatenate(cums, axis=0)

    z_parts[o_half] = _dot(h, wall_ref[:, o_half:O_SPLIT])

    vn = (v * lax.rsqrt(ms + RMS_EPS) * gv_ref[...]).astype(BF16)
    w_pairs = []
    for p in range(N_GMLP_GROUPS // 2):
        w0 = jnp.where(causal, ws_ref[2 * p], 0.0).astype(BF16)
        w1 = jnp.where(causal, ws_ref[2 * p + 1], 0.0).astype(BF16)
        w_pairs.append(jnp.concatenate([w0, w1], axis=0))
    for cp in range(n_chunks // 2):
        rows0 = slice(2 * cp * CHUNK, (2 * cp + 1) * CHUNK)
        rows1 = slice((2 * cp + 1) * CHUNK, (2 * cp + 2) * CHUNK)
        parts0, parts1 = [], []
        for p in range(N_GMLP_GROUPS // 2):
            cols = slice(p * LANES, (p + 1) * LANES)
            r = _dot(w_pairs[p], jnp.concatenate([vn[rows0, cols], vn[rows1, cols]], axis=1))
            parts0.append(jnp.where(lane < HEAD_DIM, r[:CHUNK, :LANES], r[CHUNK:, :LANES]))
            parts1.append(jnp.where(lane < HEAD_DIM, r[:CHUNK, LANES:], r[CHUNK:, LANES:]))
        oa_ref[0, rows0, :] = (u[rows0] * (jnp.concatenate(parts0, axis=1) + bs_ref[...])).astype(BF16)
        oa_ref[0, rows1, :] = (u[rows1] * (jnp.concatenate(parts1, axis=1) + bs_ref[...])).astype(BF16)

    for p in range(N_MEM_HEADS // 2):
        vmp = kv[:, D_MEMQ + p * LANES:D_MEMQ + (p + 1) * LANES]
        outs = []
        for e in range(2):
            s = mem_scores[2 * p + e]
            pe = jnp.exp(s - jnp.max(s, axis=-1, keepdims=True))
            den = jnp.sum(pe, axis=-1, keepdims=True)
            outs.append(_dot(pe.astype(BF16), vmp) / den)
        oc_ref[0, :, p * LANES:(p + 1) * LANES] = jnp.where(lo_half_t, outs[0], outs[1]).astype(BF16)

    pieces = _split_bf16(jnp.where(lane_t < N_FOX_HEADS, c_all * LOG2E, 0.0))
    packed = pieces[0].astype(F32)
    for i in range(1, N_SPLIT):
        packed = packed + pltpu.roll(pieces[i].astype(F32), PIECE_STRIDE * i, axis=1)
    extras = _dot(packed.astype(BF16), place_ref[...])
    neg_one = jnp.full((tm, LANES), -1.0, F32)
    one = jnp.ones((tm, LANES), F32)
    zero = jnp.zeros((tm, LANES), F32)
    q_fill_even = jnp.where((lane_t >= HEAD_DIM) & (lane_t < HEAD_DIM + N_SPLIT), neg_one, zero)
    q_fill_odd = jnp.where(lane_t < N_SPLIT, neg_one, zero)
    v_fill_even = jnp.where(lane_t == HEAD_DIM, one, zero)
    v_fill_odd = jnp.where(lane_t == 0, one, zero)
    for p in range(N_FOX_HEADS // 2):
        cols = slice(p * LANES, (p + 1) * LANES)
        zq = fox_cols(p * LANES) * (QK_SCALE * LOG2E)
        zk = fox_cols(D_FOX + p * LANES)
        zv = fox_cols(2 * D_FOX + p * LANES)
        ex = extras[:, cols]
        q_ref[0, 2 * p] = jnp.where(lo_half_t, zq, q_fill_even).astype(BF16)
        q_ref[0, 2 * p + 1] = jnp.where(lo_half_t, q_fill_odd, zq).astype(BF16)
        k_ref[0, 2 * p] = jnp.where(lo_half_t, zk, ex).astype(BF16)
        k_ref[0, 2 * p + 1] = jnp.where(lo_half_t, ex, zk).astype(BF16)
        v_ref[0, 2 * p] = jnp.where(lo_half_t, zv, v_fill_even).T.astype(BF16)
        v_ref[0, 2 * p + 1] = jnp.where(lo_half_t, v_fill_odd, zv).T.astype(BF16)


def _placement_matrix():
    assert N_FOX_HEADS <= PIECE_STRIDE and N_SPLIT * PIECE_STRIDE <= LANES
    m = np.zeros((LANES, D_FOX), np.float32)
    for h in range(N_FOX_HEADS):
        base = (h // 2) * LANES + (HEAD_DIM if h % 2 == 0 else 0)
        for i in range(N_SPLIT):
            m[i * PIECE_STRIDE + h, base + i] = 1.0
    return m


def _group_mean_matrix():
    g = np.arange(D_GMLP) // HEAD_DIM
    return (g[:, None] == g[None, :]).astype(np.float32) / HEAD_DIM


def _mix_proj(x, layer, kv_all, p):
    b, s, d = x.shape
    tm = min(TM_PROJ, s)
    assert s % tm == 0 and tm % CHUNK == 0
    m = kv_all.shape[2]
    const2 = lambda shape: pl.BlockSpec(shape, lambda i, j: (0, 0))
    per_layer = lambda *shape: pl.BlockSpec((None,) + shape, lambda i, j: (layer,) + (0,) * len(shape))
    head_spec = pl.BlockSpec((1, N_FOX_HEADS, tm, LANES), lambda i, j: (i, 0, j, 0))
    head_shape = jax.ShapeDtypeStruct((b, N_FOX_HEADS, s, LANES), BF16)
    return pl.pallas_call(
        functools.partial(_mix_proj_kernel, tm=tm),
        grid=(b, s // tm),
        in_specs=[
            pl.BlockSpec((1, tm, d), lambda i, j: (i, j, 0)),
            per_layer(1, d),
            per_layer(d, W_ALL),
            per_layer(1, LANES),
            per_layer(1, D_GMLP),
            per_layer(N_GMLP_GROUPS, CHUNK, CHUNK),
            per_layer(CHUNK, D_GMLP),
            pl.BlockSpec((1, 1, m, 2 * D_MEMQ), lambda i, j: (layer, i, 0, 0)),
            const2((D_GMLP, D_GMLP)),
            const2((LANES, D_FOX)),
        ],
        out_specs=[
            pl.BlockSpec((1, tm, D_GMLP), lambda i, j: (i, j, 0)),
            pl.BlockSpec((1, tm, D_MEMQ), lambda i, j: (i, j, 0)),
            head_spec, head_spec,
            pl.BlockSpec((1, N_FOX_HEADS, LANES, tm), lambda i, j: (i, 0, 0, j)),
        ],
        out_shape=[
            jax.ShapeDtypeStruct((b, s, D_GMLP), BF16),
            jax.ShapeDtypeStruct((b, s, D_MEMQ), BF16),
            head_shape, head_shape,
            jax.ShapeDtypeStruct((b, N_FOX_HEADS, LANES, s), BF16),
        ],
        scratch_shapes=[pltpu.VMEM((1, LANES), F32)],
        compiler_params=pltpu.CompilerParams(
            dimension_semantics=("arbitrary", "arbitrary"), vmem_limit_bytes=VMEM_LIMIT),
        name="mix_proj",
    )(x, p["g_pre_mix"], p["w_all"], p["b_forget"], p["g_v"], p["w_s"], p["b_s"], kv_all, p["gmean"], p["place"])


def _fox_attn_kernel(q_ref, k_ref, vt_ref, o_ref, s0_ref, s1_ref, bm0_ref, bm1_ref, m_ref, acc_ref, *, tq):
    hps = q_ref.shape[1]
    n_q = q_ref.shape[2] // tq
    sub = lax.broadcasted_iota(jnp.int32, (LANES, tq), 0)
    bufs = ((s0_ref, bm0_ref), (s1_ref, bm1_ref))

    def scores(i, j, buf, hh):
        dst_ref, bm_ref = buf
        start = pl.multiple_of(j * tq, tq)
        st = _dot_nt(k_ref[0, hh, pl.ds(start, tq), :], q_ref[0, hh, i * tq:(i + 1) * tq, :])
        dst_ref[hh] = st
        bm_ref[hh] = jnp.max(st, axis=0, keepdims=True)

    def consume(j, buf, hh):
        src_ref, bm_ref = buf
        start = pl.multiple_of(j * tq, tq)
        m = m_ref[hh]
        m_new = jnp.maximum(m, bm_ref[hh])
        alpha = jnp.exp2(m - m_new)
        pt = jnp.exp2(src_ref[hh] - m_new).astype(BF16)
        acc_ref[hh] = alpha * acc_ref[hh] + _dot(vt_ref[0, hh, :, pl.ds(start, tq)], pt)
        m_ref[hh] = m_new

    h = tq // 2
    tri = lax.broadcasted_iota(jnp.int32, (h, h), 0) <= lax.broadcasted_iota(jnp.int32, (h, h), 1)

    def scores_diag(i, buf, hh):
        dst_ref, bm_ref = buf
        base = i * tq
        k_lo, k_hi = k_ref[0, hh, base:base + h, :], k_ref[0, hh, base + h:base + tq, :]
        q_lo, q_hi = q_ref[0, hh, base:base + h, :], q_ref[0, hh, base + h:base + tq, :]
        s_ll = jnp.where(tri, _dot_nt(k_lo, q_lo), NEG_INF)
        s_lh = _dot_nt(k_lo, q_hi)
        s_hh = jnp.where(tri, _dot_nt(k_hi, q_hi), NEG_INF)
        dst_ref[hh, :h, :h] = s_ll
        dst_ref[hh, :h, h:] = s_lh
        dst_ref[hh, h:, h:] = s_hh
        bm_ref[hh, :, :h] = jnp.max(s_ll, axis=0, keepdims=True)
        bm_ref[hh, :, h:] = jnp.maximum(jnp.max(s_lh, axis=0, keepdims=True), jnp.max(s_hh, axis=0, keepdims=True))

    def consume_diag(i, buf, hh):
        src_ref, bm_ref = buf
        base = i * tq
        s_ll, s_lh, s_hh = src_ref[hh, :h, :h], src_ref[hh, :h, h:], src_ref[hh, h:, h:]
        m_lo, m_hi = m_ref[hh, :, :h], m_ref[hh, :, h:]
        new_lo = jnp.maximum(m_lo, bm_ref[hh, :, :h])
        new_hi = jnp.maximum(m_hi, bm_ref[hh, :, h:])
        p_ll = jnp.exp2(s_ll - new_lo).astype(BF16)
        p_hi = jnp.concatenate([jnp.exp2(s_lh - new_hi), jnp.exp2(s_hh - new_hi)], axis=0).astype(BF16)
        acc_ref[hh, :, :h] = (jnp.exp2(m_lo - new_lo) * acc_ref[hh, :, :h]
                              + _dot(vt_ref[0, hh, :, base:base + h], p_ll))
        acc_ref[hh, :, h:] = (jnp.exp2(m_hi - new_hi) * acc_ref[hh, :, h:]
                              + _dot(vt_ref[0, hh, :, base:base + tq], p_hi))
        m_ref[hh, :, :h] = new_lo
        m_ref[hh, :, h:] = new_hi

    def stage(issue, retire):
        for hh in range(hps):
            if issue is not None:
                issue(hh)
            retire(hh)

    def finish_tile(i):
        for p in range(hps // 2):
            outs = []
            for e in range(2):
                acc = acc_ref[2 * p + e]
                den_row = HEAD_DIM if e == 0 else 0
                outs.append(acc / acc[den_row:den_row + 1, :])
            pair_t = jnp.where(sub < HEAD_DIM, outs[0], outs[1])
            o_ref[0, i * tq:(i + 1) * tq, p * LANES:(p + 1) * LANES] = pair_t.T.astype(BF16)
        m_ref[...] = jnp.full(m_ref.shape, NEG_INF, F32)
        acc_ref[...] = jnp.zeros(acc_ref.shape, F32)

    m_ref[...] = jnp.full(m_ref.shape, NEG_INF, F32)
    acc_ref[...] = jnp.zeros(acc_ref.shape, F32)
    cur = 0
    P = functools.partial
    for hh in range(hps):
        scores_diag(0, bufs[0], hh)
    for i in range(n_q):
        x_ref, y_ref = bufs[cur], bufs[1 - cur]
        first_of_next = P(scores, i + 1, 0) if i + 1 < n_q else None

        def two_blocks(p, _, i=i, x_ref=x_ref, y_ref=y_ref):
            stage(P(scores, i, 2 * p + 1, y_ref), P(consume, 2 * p, x_ref))
            stage(P(scores, i, 2 * p + 2, x_ref), P(consume, 2 * p + 1, y_ref))
            return 0

        if i % 2 == 0:
            if i >= 2:
                if i >= 4:
                    lax.fori_loop(0, i // 2 - 1, two_blocks, 0)
                stage(P(scores, i, i - 1, y_ref), P(consume, i - 2, x_ref))
                stage(P(scores_diag, i, x_ref), P(consume, i - 1, y_ref))
            stage(P(first_of_next, y_ref) if first_of_next else None, P(consume_diag, i, x_ref))
            cur = 1 - cur
        else:
            if i >= 3:
                lax.fori_loop(0, i // 2, two_blocks, 0)
            stage(P(scores_diag, i, y_ref), P(consume, i - 1, x_ref))
            stage(P(first_of_next, x_ref) if first_of_next else None, P(consume_diag, i, y_ref))
        finish_tile(i)


def _fox_attn(q, k, v):
    b, nh, s, _ = q.shape
    tq = min(TQ, s)
    hps = HEADS_PER_STEP
    assert s % tq == 0 and nh % hps == 0 and hps % 2 == 0
    qk_spec = pl.BlockSpec((1, hps, s, LANES), lambda i, p: (i, p, 0, 0))
    vt_spec = pl.BlockSpec((1, hps, LANES, s), lambda i, p: (i, p, 0, 0))
    return pl.pallas_call(
        functools.partial(_fox_attn_kernel, tq=tq),
        grid=(b, nh // hps),
        in_specs=[qk_spec, qk_spec, vt_spec],
        out_specs=pl.BlockSpec((1, s, hps * HEAD_DIM), lambda i, p: (i, 0, p)),
        out_shape=jax.ShapeDtypeStruct((b, s, nh * HEAD_DIM), BF16),
        scratch_shapes=[
            pltpu.VMEM((hps, tq, tq), F32), pltpu.VMEM((hps, tq, tq), F32),
            pltpu.VMEM((hps, 1, tq), F32), pltpu.VMEM((hps, 1, tq), F32),
            pltpu.VMEM((hps, 1, tq), F32), pltpu.VMEM((hps, LANES, tq), F32),
        ],
        compiler_params=pltpu.CompilerParams(
            dimension_semantics=("arbitrary", "arbitrary"), vmem_limit_bytes=VMEM_LIMIT),
        name="fox_attn",
    )(q, k, v)


def _out_ffn_kernel(x_ref, a_ref, b_ref, c_ref, wo_ref, g1_ref, g2_ref, g3_ref,
                    w1_ref, w2_ref, o_ref, *, ff_chunk, n_sub):
    tm = x_ref.shape[0]
    sub = tm // n_sub
    d_ff = w1_ref.shape[1]
    ys = []
    for s in range(n_sub):
        rows = slice(s * sub, (s + 1) * sub)
        mixed = jnp.concatenate([a_ref[rows, :], b_ref[rows, :], c_ref[rows, :]], axis=1)
        ys.append(_dot(mixed, wo_ref[...]))
    for s in range(n_sub):
        rows = slice(s * sub, (s + 1) * sub)
        x1 = x_ref[rows, :] + _rmsnorm(ys[s], g1_ref[...])
        h = _rmsnorm(x1, g2_ref[...]).astype(BF16)
        f = None
        for c in range(d_ff // ff_chunk):
            cols = slice(c * ff_chunk, (c + 1) * ff_chunk)
            a1 = jnp.maximum(_dot(h, w1_ref[:, cols]), 0.0)
            part = _dot((a1 * a1).astype(BF16), w2_ref[cols, :])
            f = part if f is None else f + part
        o_ref[rows, :] = x1 + _rmsnorm(f, g3_ref[...])


def _out_ffn(x, a, b_, c, layer, p):
    t, d = x.shape
    tm = min(TM_FFN, t)
    assert t % tm == 0
    d_ff = p["w_ff1"].shape[2]
    ff_chunk = min(FF_CHUNK, d_ff)
    row = lambda w: pl.BlockSpec((tm, w), lambda i: (i, 0))
    per_layer = lambda shape: pl.BlockSpec(
        (None,) + shape, lambda i: (layer,) + (0,) * len(shape), pipeline_mode=pl.Buffered(1))
    return pl.pallas_call(
        functools.partial(_out_ffn_kernel, ff_chunk=ff_chunk, n_sub=max(1, tm // FFN_SUB_ROWS)),
        grid=(t // tm,),
        in_specs=[
            row(d), row(D_GMLP), row(D_FOX), row(D_MEMQ),
            per_layer((D_GMLP + D_FOX + D_MEMQ, d)),
            per_layer((1, d)), per_layer((1, d)), per_layer((1, d)),
            per_layer((d, d_ff)), per_layer((d_ff, d)),
        ],
        out_specs=row(d),
        out_shape=jax.ShapeDtypeStruct((t, d), F32),
        compiler_params=pltpu.CompilerParams(
            dimension_semantics=("arbitrary",), vmem_limit_bytes=VMEM_LIMIT),
        name="out_ffn",
    )(x, a, b_, c, p["w_out"], p["g_post_mix"], p["g_pre_ffn"], p["g_post_ffn"],
      p["w_ff1"], p["w_ff2"])


def _pack_params(norm_pre_mix, norm_post_mix, norm_pre_ffn, norm_post_ffn, w_in, b_forget,
                 gmlp_v_norm, gmlp_w_s, gmlp_b_s, w_out, w_ff1, w_ff2):
    depth, d, _ = w_in.shape
    i_mq = O_GATE + N_FOX_HEADS
    pad = LANES - N_FOX_HEADS
    w_all = jnp.concatenate(
        [w_in[:, :, :i_mq].astype(BF16), jnp.zeros((depth, d, pad), BF16), w_in[:, :, i_mq:].astype(BF16)], axis=2)
    return {
        "g_pre_mix": norm_pre_mix.reshape(depth, 1, d),
        "w_all": w_all,
        "b_forget": jnp.pad(b_forget, ((0, 0), (0, pad))).reshape(depth, 1, LANES),
        "g_v": gmlp_v_norm.reshape(depth, 1, D_GMLP),
        "w_s": gmlp_w_s,
        "b_s": jnp.repeat(jnp.swapaxes(gmlp_b_s, 1, 2), HEAD_DIM, axis=2),
        "gmean": jnp.asarray(_group_mean_matrix(), BF16),
        "place": jnp.asarray(_placement_matrix(), BF16),
        "w_out": w_out.astype(BF16),
        "g_post_mix": norm_post_mix.reshape(depth, 1, d),
        "g_pre_ffn": norm_pre_ffn.reshape(depth, 1, d),
        "g_post_ffn": norm_post_ffn.reshape(depth, 1, d),
        "w_ff1": w_ff1.astype(BF16),
        "w_ff2": w_ff2.astype(BF16),
    }


def kernel(x, mem, norm_pre_mix, norm_post_mix, norm_pre_ffn, norm_post_ffn, norm_mem, w_in, b_forget,
           gmlp_v_norm, gmlp_w_s, gmlp_b_s, w_mem_kv, w_out, w_ff1, w_ff2):
    b, s, d = x.shape
    depth = w_in.shape[0]
    p = _pack_params(norm_pre_mix, norm_post_mix, norm_pre_ffn, norm_post_ffn, w_in, b_forget,
                     gmlp_v_norm, gmlp_w_s, gmlp_b_s, w_out, w_ff1, w_ff2)
    kv_all = _memkv(mem, norm_mem, w_mem_kv.astype(BF16))
    for l in range(depth):
        out_a, out_c, q, k, v = _mix_proj(x, l, kv_all, p)
        out_b = _fox_attn(q, k, v)
        x = _out_ffn(x.reshape(b * s, d), out_a.reshape(b * s, D_GMLP), out_b.reshape(b * s, D_FOX),
                     out_c.reshape(b * s, D_MEMQ), l, p).reshape(b, s, d)
    return x
```

```python
import functools
import math

import numpy as np
import jax
import jax.numpy as jnp
from jax import lax
from jax.experimental import pallas as pl
from jax.experimental.pallas import tpu as pltpu

F32 = jnp.float32
BF16 = jnp.bfloat16

HEAD_DIM = 64
LANES = 128
N_GMLP_GROUPS = 6
N_FOX_HEADS = 6
N_MEM_HEADS = 4
D_GMLP = N_GMLP_GROUPS * HEAD_DIM
D_FOX = N_FOX_HEADS * HEAD_DIM
D_MEMQ = N_MEM_HEADS * HEAD_DIM
CHUNK = 128
RMS_EPS = 1e-6
NEG_INF = -1e30
QK_SCALE = HEAD_DIM ** -0.5
LOG2E = math.log2(math.e)
N_SPLIT = 3
PIECE_STRIDE = 8
O_FX = 2 * D_GMLP
O_GATE = O_FX + 3 * D_FOX
O_MQ = O_GATE + LANES
W_ALL = O_MQ + D_MEMQ
MXU_TILE = 256
O_SPLIT = O_GATE - (O_GATE - O_FX) % MXU_TILE
assert O_FX % MXU_TILE == 0 and (O_SPLIT - O_FX) % MXU_TILE == 0 and (W_ALL - O_SPLIT) % MXU_TILE == 0

V7X_VMEM_BYTES = 64 * 1024 * 1024
VMEM_LIMIT = V7X_VMEM_BYTES - 8 * 1024 * 1024

TM_PROJ = 1024
TQ = 512
HEADS_PER_STEP = 2
TM_FFN = 1024
FFN_SUB_ROWS = 256
FF_CHUNK = 1024


def _dot(a, b):
    return jnp.dot(a, b, preferred_element_type=F32)


def _dot_nt(a, b):
    return lax.dot_general(a, b, (((1,), (1,)), ((), ())), preferred_element_type=F32)


def _rmsnorm(x, g):
    return x * lax.rsqrt(jnp.mean(x * x, axis=-1, keepdims=True) + RMS_EPS) * g


def _gelu_tanh(x):
    c = math.sqrt(2.0 / math.pi)
    return 0.5 * x * (1.0 + jnp.tanh(c * (x + 0.044715 * (x * x * x))))


def _log_sigmoid(x):
    return -(jnp.maximum(-x, 0.0) + jnp.log1p(jnp.exp(-jnp.abs(x))))


def _split_bf16(x):
    pieces = []
    r = x
    for _ in range(N_SPLIT):
        p = r.astype(BF16)
        pieces.append(p)
        r = r - p.astype(F32)
    return pieces


def _mix_proj_kernel(x_ref, gpre_ref, wall_ref, bf_ref, gv_ref, ws_ref, bs_ref, mem_ref, gmem_ref, wkv_ref,
                     gmean_ref, place_ref, oa_ref, oc_ref, q_ref, k_ref, v_ref, carry_ref, kv_ref, *, tm):
    n_chunks = tm // CHUNK
    assert n_chunks % 2 == 0

    @pl.when(pl.program_id(1) == 0)
    def _():
        carry_ref[...] = jnp.zeros_like(carry_ref)
        mem_n = _rmsnorm(mem_ref[0], gmem_ref[...]).astype(BF16)
        kv_ref[...] = _dot(mem_n, wkv_ref[...]).astype(BF16)

    lane =lax.broadcasted_iota(jnp.int32, (CHUNK, LANES), 1)
    lane_t = lax.broadcasted_iota(jnp.int32, (tm, LANES), 1)
    lo_half_t = lane_t < HEAD_DIM
    row = lax.broadcasted_iota(jnp.int32, (CHUNK, CHUNK), 0)
    col = lax.broadcasted_iota(jnp.int32, (CHUNK, CHUNK), 1)
    causal = row >= col

    h = _rmsnorm(x_ref[0], gpre_ref[...]).astype(BF16)
    zg = _dot(h, wall_ref[:, :O_FX])
    z_tail = _dot(h, wall_ref[:, O_SPLIT:])
    o_half = O_FX + (O_SPLIT - O_FX) // 2
    assert (o_half - O_FX) % MXU_TILE == 0
    z_parts = {O_FX: _dot(h, wall_ref[:, O_FX:o_half]), O_SPLIT: z_tail}
    zgate = z_tail[:, O_GATE - O_SPLIT:O_MQ - O_SPLIT]
    zm = z_tail[:, O_MQ - O_SPLIT:] * QK_SCALE

    def fox_cols(start):
        col = O_FX + start
        base = max(b for b in z_parts if b <= col)
        return z_parts[base][:, col - base:col - base + LANES]

    u = _gelu_tanh(zg[:, :D_GMLP])
    v = _gelu_tanh(zg[:, D_GMLP:])
    sq = (v * v).astype(BF16)
    split = (D_GMLP // MXU_TILE) * MXU_TILE
    ms = jnp.concatenate([_dot(sq[:, :split], gmean_ref[:split, :split]),
                          _dot(sq[:, split:], gmean_ref[split:, split:])], axis=1)

    kv = kv_ref[...]
    mem_scores = []
    for hd in range(N_MEM_HEADS):
        p, e = divmod(hd, 2)
        keep = lo_half_t if e == 0 else jnp.logical_not(lo_half_t)
        qh = jnp.where(keep, zm[:, p * LANES:(p + 1) * LANES], 0.0).astype(BF16)
        mem_scores.append(_dot_nt(qh, kv[:, p * LANES:(p + 1) * LANES]))

    log_f = _log_sigmoid(zgate + bf_ref[...])
    tril = jnp.where(causal, 1.0, 0.0).astype(BF16)
    carry = carry_ref[...]
    cums = []
    for c in range(n_chunks):
        pieces = _split_bf16(log_f[c * CHUNK:(c + 1) * CHUNK])
        cum3 = _dot(tril, jnp.concatenate(pieces, axis=1))
        cum = carry
        for i in range(N_SPLIT):
            cum = cum + cum3[:, i * LANES:(i + 1) * LANES]
        carry = cum[CHUNK - 1:CHUNK, :]
        cums.append(cum)
    carry_ref[...] = carry
    c_all = jnp.concatenate(cums, axis=0)

    z_parts[o_half] = _dot(h, wall_ref[:, o_half:O_SPLIT])

    vn = (v * lax.rsqrt(ms + RMS_EPS) * gv_ref[...]).astype(BF16)
    w_pairs = []
    for p in range(N_GMLP_GROUPS // 2):
        w0 = jnp.where(causal, ws_ref[2 * p], 0.0).astype(BF16)
        w1 = jnp.where(causal, ws_ref[2 * p + 1], 0.0).astype(BF16)
        w_pairs.append(jnp.concatenate([w0, w1], axis=0))
    for cp in range(n_chunks // 2):
        rows0 = slice(2 * cp * CHUNK, (2 * cp + 1) * CHUNK)
        rows1 = slice((2 * cp + 1) * CHUNK, (2 * cp + 2) * CHUNK)
        parts0, parts1 = [], []
        for p in range(N_GMLP_GROUPS // 2):
            cols = slice(p * LANES, (p + 1) * LANES)
            r = _dot(w_pairs[p], jnp.concatenate([vn[rows0, cols], vn[rows1, cols]], axis=1))
            parts0.append(jnp.where(lane < HEAD_DIM, r[:CHUNK, :LANES], r[CHUNK:, :LANES]))
            parts1.append(jnp.where(lane < HEAD_DIM, r[:CHUNK, LANES:], r[CHUNK:, LANES:]))
        oa_ref[0, rows0, :] = (u[rows0] * (jnp.concatenate(parts0, axis=1) + bs_ref[...])).astype(BF16)
        oa_ref[0, rows1, :] = (u[rows1] * (jnp.concatenate(parts1, axis=1) + bs_ref[...])).astype(BF16)

    for p in range(N_MEM_HEADS // 2):
        vmp = kv[:, D_MEMQ + p * LANES:D_MEMQ + (p + 1) * LANES]
        outs = []
        for e in range(2):
            s = mem_scores[2 * p + e]
            pe = jnp.exp(s - jnp.max(s, axis=-1, keepdims=True))
            den = jnp.sum(pe, axis=-1, keepdims=True)
            outs.append(_dot(pe.astype(BF16), vmp) / den)
        oc_ref[0, :, p * LANES:(p + 1) * LANES] = jnp.where(lo_half_t, outs[0], outs[1]).astype(BF16)

    pieces = _split_bf16(jnp.where(lane_t < N_FOX_HEADS, c_all * LOG2E, 0.0))
    packed = pieces[0].astype(F32)
    for i in range(1, N_SPLIT):
        packed = packed + pltpu.roll(pieces[i].astype(F32), PIECE_STRIDE * i, axis=1)
    extras = _dot(packed.astype(BF16), place_ref[...])
    neg_one = jnp.full((tm, LANES), -1.0, F32)
    one = jnp.ones((tm, LANES), F32)
    zero = jnp.zeros((tm, LANES), F32)
    q_fill_even = jnp.where((lane_t >= HEAD_DIM) & (lane_t < HEAD_DIM + N_SPLIT), neg_one, zero)
    q_fill_odd = jnp.where(lane_t < N_SPLIT, neg_one, zero)
    v_fill_even = jnp.where(lane_t == HEAD_DIM, one, zero)
    v_fill_odd = jnp.where(lane_t == 0, one, zero)
    for p in range(N_FOX_HEADS // 2):
        cols = slice(p * LANES, (p + 1) * LANES)
        zq = fox_cols(p * LANES) * (QK_SCALE * LOG2E)
        zk = fox_cols(D_FOX + p * LANES)
        zv = fox_cols(2 * D_FOX + p * LANES)
        ex = extras[:, cols]
        q_ref[0, 2 * p] = jnp.where(lo_half_t, zq, q_fill_even).astype(BF16)
        q_ref[0, 2 * p + 1] = jnp.where(lo_half_t, q_fill_odd, zq).astype(BF16)
        k_ref[0, 2 * p] = jnp.where(lo_half_t, zk, ex).astype(BF16)
        k_ref[0, 2 * p + 1] = jnp.where(lo_half_t, ex, zk).astype(BF16)
        v_ref[0, 2 * p] = jnp.where(lo_half_t, zv, v_fill_even).T.astype(BF16)
        v_ref[0, 2 * p + 1] = jnp.where(lo_half_t, v_fill_odd, zv).T.astype(BF16)


def _placement_matrix():
    assert N_FOX_HEADS <= PIECE_STRIDE and N_SPLIT * PIECE_STRIDE <= LANES
    m = np.zeros((LANES, D_FOX), np.float32)
    for h in range(N_FOX_HEADS):
        base = (h // 2) * LANES + (HEAD_DIM if h % 2 == 0 else 0)
        for i in range(N_SPLIT):
            m[i * PIECE_STRIDE + h, base + i] = 1.0
    return m


def _group_mean_matrix():
    g = np.arange(D_GMLP) // HEAD_DIM
    return (g[:, None] == g[None, :]).astype(np.float32) / HEAD_DIM


def _mix_proj(x, mem, layer, p):
    b, s, d = x.shape
    tm = min(TM_PROJ, s)
    assert s % tm == 0 and tm % CHUNK == 0
    m = mem.shape[1]
    const2 = lambda shape: pl.BlockSpec(shape, lambda i, j: (0, 0))
    per_layer = lambda *shape: pl.BlockSpec((None,) + shape, lambda i, j: (layer,) + (0,) * len(shape))
    head_spec = pl.BlockSpec((1, N_FOX_HEADS, tm, LANES), lambda i, j: (i, 0, j, 0))
    head_shape = jax.ShapeDtypeStruct((b, N_FOX_HEADS, s, LANES), BF16)
    return pl.pallas_call(
        functools.partial(_mix_proj_kernel, tm=tm),
        grid=(b, s // tm),
        in_specs=[
            pl.BlockSpec((1, tm, d), lambda i, j: (i, j, 0)),
            per_layer(1, d),
            per_layer(d, W_ALL),
            per_layer(1, LANES),
            per_layer(1, D_GMLP),
            per_layer(N_GMLP_GROUPS, CHUNK, CHUNK),
            per_layer(CHUNK, D_GMLP),
            pl.BlockSpec((1, m, d), lambda i, j: (i, 0, 0)),
            per_layer(1, d),
            per_layer(d, 2 * D_MEMQ),
            const2((D_GMLP, D_GMLP)),
            const2((LANES, D_FOX)),
        ],
        out_specs=[
            pl.BlockSpec((1, tm, D_GMLP), lambda i, j: (i, j, 0)),
            pl.BlockSpec((1, tm, D_MEMQ), lambda i, j: (i, j, 0)),
            head_spec, head_spec,
            pl.BlockSpec((1, N_FOX_HEADS, LANES, tm), lambda i, j: (i, 0, 0, j)),
        ],
        out_shape=[
            jax.ShapeDtypeStruct((b, s, D_GMLP), BF16),
            jax.ShapeDtypeStruct((b, s, D_MEMQ), BF16),
            head_shape, head_shape,
            jax.ShapeDtypeStruct((b, N_FOX_HEADS, LANES, s), BF16),
        ],
        scratch_shapes=[pltpu.VMEM((1, LANES), F32),
                        pltpu.VMEM((m, 2 * D_MEMQ), BF16)],
        compiler_params=pltpu.CompilerParams(
            dimension_semantics=("arbitrary", "arbitrary"), vmem_limit_bytes=VMEM_LIMIT),
        name="mix_proj",
    )(x, p["g_pre_mix"], p["w_all"], p["b_forget"], p["g_v"], p["w_s"], p["b_s"], mem, p["g_mem"], p["w_kv"],
      p["gmean"], p["place"])


def _fox_attn_kernel(q_ref, k_ref, vt_ref, o_ref, s0_ref, s1_ref, bm0_ref, bm1_ref, m_ref, acc_ref, *, tq):
    hps = q_ref.shape[1]
    n_q = q_ref.shape[2] // tq
    sub = lax.broadcasted_iota(jnp.int32, (LANES, tq), 0)
    bufs = ((s0_ref, bm0_ref), (s1_ref, bm1_ref))

    def scores(i, j, buf, hh):
        dst_ref, bm_ref = buf
        start = pl.multiple_of(j * tq, tq)
        st = _dot_nt(k_ref[0, hh, pl.ds(start, tq), :], q_ref[0, hh, i * tq:(i + 1) * tq, :])
        dst_ref[hh] = st
        bm_ref[hh] = jnp.max(st, axis=0, keepdims=True)

    def consume(j, buf, hh):
        src_ref, bm_ref = buf
        start = pl.multiple_of(j * tq, tq)
        m = m_ref[hh]
        m_new = jnp.maximum(m, bm_ref[hh])
        alpha = jnp.exp2(m - m_new)
        pt = jnp.exp2(src_ref[hh] - m_new).astype(BF16)
        acc_ref[hh] = alpha * acc_ref[hh] + _dot(vt_ref[0, hh, :, pl.ds(start, tq)], pt)
        m_ref[hh] = m_new

    h = tq // 2
    tri = lax.broadcasted_iota(jnp.int32, (h, h), 0) <= lax.broadcasted_iota(jnp.int32, (h, h), 1)

    def scores_diag(i, buf, hh):
        dst_ref, bm_ref = buf
        base = i * tq
        k_lo, k_hi = k_ref[0, hh, base:base + h, :], k_ref[0, hh, base + h:base + tq, :]
        q_lo, q_hi = q_ref[0, hh, base:base + h, :], q_ref[0, hh, base + h:base + tq, :]
        s_ll = jnp.where(tri, _dot_nt(k_lo, q_lo), NEG_INF)
        s_lh = _dot_nt(k_lo, q_hi)
        s_hh = jnp.where(tri, _dot_nt(k_hi, q_hi), NEG_INF)
        dst_ref[hh, :h, :h] = s_ll
        dst_ref[hh, :h, h:] = s_lh
        dst_ref[hh, h:, h:] = s_hh
        bm_ref[hh, :, :h] = jnp.max(s_ll, axis=0, keepdims=True)
        bm_ref[hh, :, h:] = jnp.maximum(jnp.max(s_lh, axis=0, keepdims=True), jnp.max(s_hh, axis=0, keepdims=True))

    def consume_diag(i, buf, hh):
        src_ref, bm_ref = buf
        base = i * tq
        s_ll, s_lh, s_hh = src_ref[hh, :h, :h], src_ref[hh, :h, h:], src_ref[hh, h:, h:]
        m_lo, m_hi = m_ref[hh, :, :h], m_ref[hh, :, h:]
        new_lo = jnp.maximum(m_lo, bm_ref[hh, :, :h])
        new_hi = jnp.maximum(m_hi, bm_ref[hh, :, h:])
        p_ll = jnp.exp2(s_ll - new_lo).astype(BF16)
        p_hi = jnp.concatenate([jnp.exp2(s_lh - new_hi), jnp.exp2(s_hh - new_hi)], axis=0).astype(BF16)
        acc_ref[hh, :, :h] = (jnp.exp2(m_lo - new_lo) * acc_ref[hh, :, :h]
                              + _dot(vt_ref[0, hh, :, base:base + h], p_ll))
        acc_ref[hh, :, h:] = (jnp.exp2(m_hi - new_hi) * acc_ref[hh, :, h:]
                              + _dot(vt_ref[0, hh, :, base:base + tq], p_hi))
        m_ref[hh, :, :h] = new_lo
        m_ref[hh, :, h:] = new_hi

    def stage(issue, retire):
        for hh in range(hps):
            if issue is not None:
                issue(hh)
            retire(hh)

    def finish_tile(i):
        for p in range(hps // 2):
            outs = []
            for e in range(2):
                acc = acc_ref[2 * p + e]
                den_row = HEAD_DIM if e == 0 else 0
                outs.append(acc / acc[den_row:den_row + 1, :])
            pair_t = jnp.where(sub < HEAD_DIM, outs[0], outs[1])
            o_ref[0, i * tq:(i + 1) * tq, p * LANES:(p + 1) * LANES] = pair_t.T.astype(BF16)
        m_ref[...] = jnp.full(m_ref.shape, NEG_INF, F32)
        acc_ref[...] = jnp.zeros(acc_ref.shape, F32)

    m_ref[...] = jnp.full(m_ref.shape, NEG_INF, F32)
    acc_ref[...] = jnp.zeros(acc_ref.shape, F32)
    cur = 0
    P = functools.partial
    for hh in range(hps):
        scores_diag(0, bufs[0], hh)
    for i in range(n_q):
        x_ref, y_ref = bufs[cur], bufs[1 - cur]
        first_of_next = P(scores, i + 1, 0) if i + 1 < n_q else None

        def two_blocks(p, _, i=i, x_ref=x_ref, y_ref=y_ref):
            stage(P(scores, i, 2 * p + 1, y_ref), P(consume, 2 * p, x_ref))
            stage(P(scores, i, 2 * p + 2, x_ref), P(consume, 2 * p + 1, y_ref))
            return 0

        if i % 2 == 0:
            if i >= 2:
                if i >= 4:
                    lax.fori_loop(0, i // 2 - 1, two_blocks, 0)
                stage(P(scores, i, i - 1, y_ref), P(consume, i - 2, x_ref))
                stage(P(scores_diag, i, x_ref), P(consume, i - 1, y_ref))
            stage(P(first_of_next, y_ref) if first_of_next else None, P(consume_diag, i, x_ref))
            cur = 1 - cur
        else:
            if i >= 3:
                lax.fori_loop(0, i // 2, two_blocks, 0)
            stage(P(scores_diag, i, y_ref), P(consume, i - 1, x_ref))
            stage(P(first_of_next, x_ref) if first_of_next else None, P(consume_diag, i, y_ref))
        finish_tile(i)


def _fox_attn(q, k, v):
    b, nh, s, _ = q.shape
    tq = min(TQ, s)
    hps = HEADS_PER_STEP
    assert s % tq == 0 and nh % hps == 0 and hps % 2 == 0
    qk_spec = pl.BlockSpec((1, hps, s, LANES), lambda i, p: (i, p, 0, 0))
    vt_spec = pl.BlockSpec((1, hps, LANES, s), lambda i, p: (i, p, 0, 0))
    return pl.pallas_call(
        functools.partial(_fox_attn_kernel, tq=tq),
        grid=(b, nh // hps),
        in_specs=[qk_spec, qk_spec, vt_spec],
        out_specs=pl.BlockSpec((1, s, hps * HEAD_DIM), lambda i, p: (i, 0, p)),
        out_shape=jax.ShapeDtypeStruct((b, s, nh * HEAD_DIM), BF16),
        scratch_shapes=[
            pltpu.VMEM((hps, tq, tq), F32), pltpu.VMEM((hps, tq, tq), F32),
            pltpu.VMEM((hps, 1, tq), F32), pltpu.VMEM((hps, 1, tq), F32),
            pltpu.VMEM((hps, 1, tq), F32), pltpu.VMEM((hps, LANES, tq), F32),
        ],
        compiler_params=pltpu.CompilerParams(
            dimension_semantics=("arbitrary", "arbitrary"), vmem_limit_bytes=VMEM_LIMIT),
        name="fox_attn",
    )(q, k, v)


def _out_ffn_kernel(x_ref, a_ref, b_ref, c_ref, wo_ref, g1_ref, g2_ref, g3_ref,
                    w1_ref, w2_ref, o_ref, *, ff_chunk, n_sub):
    tm = x_ref.shape[0]
    sub = tm // n_sub
    d_ff = w1_ref.shape[1]
    ys = []
    for s in range(n_sub):
        rows = slice(s * sub, (s + 1) * sub)
        mixed = jnp.concatenate([a_ref[rows, :], b_ref[rows, :], c_ref[rows, :]], axis=1)
        ys.append(_dot(mixed, wo_ref[...]))
    for s in range(n_sub):
        rows = slice(s * sub, (s + 1) * sub)
        x1 = x_ref[rows, :] + _rmsnorm(ys[s], g1_ref[...])
        h = _rmsnorm(x1, g2_ref[...]).astype(BF16)
        f = None
        for c in range(d_ff // ff_chunk):
            cols = slice(c * ff_chunk, (c + 1) * ff_chunk)
            a1 = jnp.maximum(_dot(h, w1_ref[:, cols]), 0.0)
            part = _dot((a1 * a1).astype(BF16), w2_ref[cols, :])
            f = part if f is None else f + part
        o_ref[rows, :] = x1 + _rmsnorm(f, g3_ref[...])


def _out_ffn(x, a, b_, c, layer, p):
    t, d = x.shape
    tm = min(TM_FFN, t)
    assert t % tm == 0
    d_ff = p["w_ff1"].shape[2]
    ff_chunk = min(FF_CHUNK, d_ff)
    row = lambda w: pl.BlockSpec((tm, w), lambda i: (i, 0))
    per_layer = lambda shape: pl.BlockSpec(
        (None,) + shape, lambda i: (layer,) + (0,) * len(shape), pipeline_mode=pl.Buffered(1))
    return pl.pallas_call(
        functools.partial(_out_ffn_kernel, ff_chunk=ff_chunk, n_sub=max(1, tm // FFN_SUB_ROWS)),
        grid=(t // tm,),
        in_specs=[
            row(d), row(D_GMLP), row(D_FOX), row(D_MEMQ),
            per_layer((D_GMLP + D_FOX + D_MEMQ, d)),
            per_layer((1, d)), per_layer((1, d)), per_layer((1, d)),
            per_layer((d, d_ff)), per_layer((d_ff, d)),
        ],
        out_specs=row(d),
        out_shape=jax.ShapeDtypeStruct((t, d), F32),
        compiler_params=pltpu.CompilerParams(
            dimension_semantics=("arbitrary",), vmem_limit_bytes=VMEM_LIMIT),
        name="out_ffn",
    )(x, a, b_, c, p["w_out"], p["g_post_mix"], p["g_pre_ffn"], p["g_post_ffn"],
      p["w_ff1"], p["w_ff2"])


def _pack_params(norm_pre_mix, norm_post_mix, norm_pre_ffn, norm_post_ffn, w_in, b_forget,
                 gmlp_v_norm, gmlp_w_s, gmlp_b_s, w_out, w_ff1, w_ff2):
    depth, d, _ = w_in.shape
    i_mq = O_GATE + N_FOX_HEADS
    pad = LANES - N_FOX_HEADS
    w_all = jnp.concatenate(
        [w_in[:, :, :i_mq].astype(BF16), jnp.zeros((depth, d, pad), BF16), w_in[:, :, i_mq:].astype(BF16)], axis=2)
    return {
        "g_pre_mix": norm_pre_mix.reshape(depth, 1, d),
        "w_all": w_all,
        "b_forget": jnp.pad(b_forget, ((0, 0), (0, pad))).reshape(depth, 1, LANES),
        "g_v": gmlp_v_norm.reshape(depth, 1, D_GMLP),
        "w_s": gmlp_w_s,
        "b_s": jnp.repeat(jnp.swapaxes(gmlp_b_s, 1, 2), HEAD_DIM, axis=2),
        "gmean": jnp.asarray(_group_mean_matrix(), BF16),
        "place": jnp.asarray(_placement_matrix(), BF16),
        "w_out": w_out.astype(BF16),
        "g_post_mix": norm_post_mix.reshape(depth, 1, d),
        "g_pre_ffn": norm_pre_ffn.reshape(depth, 1, d),
        "g_post_ffn": norm_post_ffn.reshape(depth, 1, d),
        "w_ff1": w_ff1.astype(BF16),
        "w_ff2": w_ff2.astype(BF16),
    }


def kernel(x, mem, norm_pre_mix, norm_post_mix, norm_pre_ffn, norm_post_ffn, norm_mem, w_in, b_forget,
           gmlp_v_norm, gmlp_w_s, gmlp_b_s, w_mem_kv, w_out, w_ff1, w_ff2):
    b, s, d = x.shape
    depth = w_in.shape[0]
    p = _pack_params(norm_pre_mix, norm_post_mix, norm_pre_ffn, norm_post_ffn, w_in, b_forget,
                     gmlp_v_norm, gmlp_w_s, gmlp_b_s, w_out, w_ff1, w_ff2)
    p["g_mem"] = norm_mem.reshape(depth, 1, d)
    p["w_kv"] = w_mem_kv.astype(BF16)
    for l in range(depth):
        out_a, out_c, q, k, v = _mix_proj(x, mem, l, p)
        out_b = _fox_attn(q, k, v)
        x = _out_ffn(x.reshape(b * s, d), out_a.reshape(b * s, D_GMLP), out_b.reshape(b * s, D_FOX),
                     out_c.reshape(b * s, D_MEMQ), l, p).reshape(b, s, d)
    return x
```
